```python
import math
import jax, jax.numpy as jnp
from jax import lax
import numpy as np

D_MODEL = 2048
BATCH = 2
SEQ = 4096
DEPTH = 4

N_A_LAYERS = DEPTH // 2
N_B_LAYERS = DEPTH - N_A_LAYERS
GM_CHUNK = 128
GM_WIDTH = D_MODEL
GM_GROUPS = 16
GM_GROUP_DIM = GM_WIDTH // GM_GROUPS
N_HEADS = 16
HEAD_DIM = D_MODEL // N_HEADS
MOBA_BLOCK = 256
MOBA_TOPK = 3
MOBA_Q_CHUNK = 16
N_EXPERTS = 32
TOP_K_EXPERTS = 4
D_EXPERT = D_MODEL // 4
SWIGLU_LIMIT = 7.0
SWIGLU_ALPHA = 1.702
MOE_TOKEN_BLOCK = 128
DEEPNORM_ALPHA = (2.0 * DEPTH) ** 0.25
DEEPNORM_BETA = (8.0 * DEPTH) ** -0.25
LN_EPS = 1e-5

kernel_name = "yoco_gmlp_moba_moe_deepnorm_adaln"


def _layer_norm(x, g, b):
    xf = x.astype(jnp.float32)
    mu = jnp.mean(xf, axis=-1, keepdims=True)
    var = jnp.mean(jnp.square(xf - mu), axis=-1, keepdims=True)
    y = (xf - mu) * lax.rsqrt(var + LN_EPS) * g.astype(jnp.float32) + b.astype(jnp.float32)
    return y.astype(x.dtype)


def _modulate(x, shift, scale):
    return x * (1.0 + scale[:, None, :]) + shift[:, None, :]


def _gmlp_chunk_mixer(h, w_in, b_in, lnv_g, lnv_b, w_s, b_s, w_out):
    B, S, _ = h.shape
    z = jax.nn.gelu(h @ w_in + b_in)
    u, v = z[..., :GM_WIDTH], z[..., GM_WIDTH:]
    v = _layer_norm(v, lnv_g, lnv_b)
    v = v.reshape(B, S // GM_CHUNK, GM_CHUNK, GM_GROUPS, GM_GROUP_DIM)
    causal = jnp.tril(jnp.ones((GM_CHUNK, GM_CHUNK), dtype=w_s.dtype))
    ws = w_s * causal[None]
    sv = jnp.einsum('gts,bnsgc->bntgc', ws, v) + b_s.T[:, :, None]
    y = u * sv.reshape(B, S, GM_WIDTH)
    return y @ w_out


def _shared_kv(x, c_act, kv_ada_w, kv_ada_b, w_kv):
    B, S, _ = x.shape
    shift, scale = jnp.split(c_act @ kv_ada_w + kv_ada_b, 2, axis=-1)
    kv = _modulate(x, shift, scale) @ w_kv
    kv = kv.reshape(B, S, 2, N_HEADS, HEAD_DIM)
    nb = -(-S // MOBA_BLOCK)
    pad = nb * MOBA_BLOCK - S
    kv = jnp.pad(kv, ((0, 0), (0, pad), (0, 0), (0, 0), (0, 0)))
    kv = kv.reshape(B, nb, MOBA_BLOCK, 2, N_HEADS, HEAD_DIM).transpose(3, 0, 4, 1, 2, 5)
    k_blocks, v_blocks = kv[0], kv[1]
    k_means = jnp.mean(k_blocks.astype(jnp.float32), axis=3).astype(k_blocks.dtype)
    return k_blocks, v_blocks, k_means


def _moba_attention(h, w_q, w_out, k_blocks, v_blocks, k_means):
    B, S, _ = h.shape
    nb = k_blocks.shape[2]
    topk = min(MOBA_TOPK, nb)
    scale = HEAD_DIM ** -0.5
    q = (h @ w_q).reshape(B, S, N_HEADS, HEAD_DIM).transpose(0, 2, 1, 3)
    k_flat = k_blocks.reshape(B * N_HEADS * nb, MOBA_BLOCK, HEAD_DIM)
    v_flat = v_blocks.reshape(B * N_HEADS * nb, MOBA_BLOCK, HEAD_DIM)
    bh = (jnp.arange(B, dtype=jnp.int32)[:, None] * N_HEADS
          + jnp.arange(N_HEADS, dtype=jnp.int32)[None, :])[:, :, None, None]
    blk_ids = jnp.arange(nb, dtype=jnp.int32)
    n_chunks = S // MOBA_Q_CHUNK

    def one_chunk(i):
        start = i * MOBA_Q_CHUNK
        own = start // MOBA_BLOCK
        qc = lax.dynamic_slice_in_dim(q, start, MOBA_Q_CHUNK, axis=2)
        gs = jnp.einsum('bhqd,bhnd->bhqn', qc, k_means, preferred_element_type=jnp.float32)
        gs = jnp.where(blk_ids < own, gs, -jnp.inf)
        _, idx = lax.top_k(gs, topk)
        valid = idx < own
        flat = bh * nb + idx
        kg = jnp.take(k_flat, flat, axis=0)
        vg = jnp.take(v_flat, flat, axis=0)
        s_sel = jnp.einsum('bhqd,bhqkjd->bhqkj', qc, kg,
                           preferred_element_type=jnp.float32) * scale
        s_sel = jnp.where(valid[..., None], s_sel, -jnp.inf)
        s_sel = s_sel.reshape(B, N_HEADS, MOBA_Q_CHUNK, topk * MOBA_BLOCK)
        k_own = lax.dynamic_index_in_dim(k_blocks, own, axis=2, keepdims=False)
        v_own = lax.dynamic_index_in_dim(v_blocks, own, axis=2, keepdims=False)
        s_own = jnp.einsum('bhqd,bhjd->bhqj', qc, k_own,
                           preferred_element_type=jnp.float32) * scale
        qpos = start + jnp.arange(MOBA_Q_CHUNK, dtype=jnp.int32)
        kpos = own * MOBA_BLOCK + jnp.arange(MOBA_BLOCK, dtype=jnp.int32)
        s_own = jnp.where(kpos[None, :] <= qpos[:, None], s_own, -jnp.inf)
        p = jax.nn.softmax(jnp.concatenate([s_sel, s_own], axis=-1), axis=-1)
        p_sel = p[..., :topk * MOBA_BLOCK].reshape(
            B, N_HEADS, MOBA_Q_CHUNK, topk, MOBA_BLOCK).astype(vg.dtype)
        p_own = p[..., topk * MOBA_BLOCK:].astype(v_own.dtype)
        out = (jnp.einsum('bhqkj,bhqkjd->bhqd', p_sel, vg)
               + jnp.einsum('bhqj,bhjd->bhqd', p_own, v_own))
        return out.astype(h.dtype)

    o = lax.map(one_chunk, jnp.arange(n_chunks, dtype=jnp.int32))
    o = o.transpose(1, 0, 3, 2, 4).reshape(B, S, N_HEADS * HEAD_DIM)
    return o @ w_out


def _routed_moe(h, w_r, b_r, w_gu, b_gu, w_down, b_down):
    B, S, D = h.shape
    t = h.reshape(B * S, D)
    logits = (t @ w_r).astype(jnp.float32) + b_r.astype(jnp.float32)
    top_v, top_i = lax.top_k(logits, TOP_K_EXPERTS)
    top_w = jax.nn.softmax(top_v, axis=-1)
    gates = jnp.sum(jax.nn.one_hot(top_i, N_EXPERTS, dtype=jnp.float32) * top_w[..., None], axis=1)
    gates = gates.astype(h.dtype)
    nblk = (B * S) // MOE_TOKEN_BLOCK

    def block(args):
        tb, gb = args
        gu = jnp.einsum('td,edf->tef', tb, w_gu) + b_gu
        g = jnp.minimum(gu[..., :D_EXPERT], SWIGLU_LIMIT)
        u = jnp.clip(gu[..., D_EXPERT:], -SWIGLU_LIMIT, SWIGLU_LIMIT)
        act = (u + 1.0) * g * jax.nn.sigmoid(SWIGLU_ALPHA * g)
        return jnp.einsum('tef,efd->td', act * gb[..., None], w_down) + gb @ b_down

    out = lax.map(block, (t.reshape(nblk, MOE_TOKEN_BLOCK, D),
                          gates.reshape(nblk, MOE_TOKEN_BLOCK, N_EXPERTS)))
    return out.reshape(B, S, D)


def _normal(key, shape, scale):
    return jax.random.normal(key, shape, jnp.float32) * scale


def setup_inputs(seed: int = 0) -> dict:
    key = jax.random.key(seed)
    ks = jax.random.split(key, 26)
    D, HD, W2 = D_MODEL, N_HEADS * HEAD_DIM, 2 * GM_WIDTH
    beta = DEEPNORM_BETA
    x = _normal(ks[0], (BATCH, SEQ, D), 1.0)
    c = _normal(ks[1], (BATCH, D), 1.0)
    ada_w = _normal(ks[2], (DEPTH, D, 6 * D), 0.1 * D ** -0.5)
    ada_b = _normal(ks[3], (DEPTH, 6 * D), 0.01)
    ln_g = 1.0 + _normal(ks[4], (DEPTH, 2, D), 0.01)
    ln_b = _normal(ks[5], (DEPTH, 2, D), 0.01)
    gm_w_in = _normal(ks[6], (N_A_LAYERS, D, W2), D ** -0.5)
    gm_b_in = _normal(ks[7], (N_A_LAYERS, W2), 0.01)
    gm_lnv_g = 1.0 + _normal(ks[8], (N_A_LAYERS, GM_WIDTH), 0.01)
    gm_lnv_b = _normal(ks[9], (N_A_LAYERS, GM_WIDTH), 0.01)
    gm_w_s = _normal(ks[10], (N_A_LAYERS, GM_GROUPS, GM_CHUNK, GM_CHUNK), 0.5 * GM_CHUNK ** -0.5)
    gm_b_s = 1.0 + _normal(ks[11], (N_A_LAYERS, GM_GROUPS, GM_CHUNK), 0.01)
    gm_w_out = _normal(ks[12], (N_A_LAYERS, GM_WIDTH, D), beta * GM_WIDTH ** -0.5)
    kv_ada_w = _normal(ks[13], (D, 2 * D), 0.1 * D ** -0.5)
    kv_ada_b = _normal(ks[14], (2 * D,), 0.01)
    w_kv = jnp.concatenate([_normal(ks[15], (D, HD), D ** -0.5),
                            _normal(ks[16], (D, HD), beta * D ** -0.5)], axis=1)
    attn_w_q = _normal(ks[17], (N_B_LAYERS, D, HD), D ** -0.5)
    attn_w_out = _normal(ks[18], (N_B_LAYERS, HD, D), beta * HD ** -0.5)
    moe_w_router = _normal(ks[19], (DEPTH, D, N_EXPERTS), D ** -0.5)
    moe_b_router = _normal(ks[20], (DEPTH, N_EXPERTS), 0.01)
    moe_w_gu = _normal(ks[21], (DEPTH, N_EXPERTS, D, 2 * D_EXPERT), D ** -0.5)
    moe_b_gu = _normal(ks[22], (DEPTH, N_EXPERTS, 2 * D_EXPERT), 0.01)
    moe_w_down = _normal(ks[23], (DEPTH, N_EXPERTS, D_EXPERT, D), beta * D_EXPERT ** -0.5)
    moe_b_down = _normal(ks[24], (DEPTH, N_EXPERTS, D), 0.01)
    return {"x": x, "c": c, "ada_w": ada_w, "ada_b": ada_b, "ln_g": ln_g, "ln_b": ln_b,
            "gm_w_in": gm_w_in, "gm_b_in": gm_b_in, "gm_lnv_g": gm_lnv_g, "gm_lnv_b": gm_lnv_b,
            "gm_w_s": gm_w_s, "gm_b_s": gm_b_s, "gm_w_out": gm_w_out,
            "kv_ada_w": kv_ada_w, "kv_ada_b": kv_ada_b, "w_kv": w_kv,
            "attn_w_q": attn_w_q, "attn_w_out": attn_w_out,
            "moe_w_router": moe_w_router, "moe_b_router": moe_b_router,
            "moe_w_gu": moe_w_gu, "moe_b_gu": moe_b_gu,
            "moe_w_down": moe_w_down, "moe_b_down": moe_b_down}


def reference(x, c, ada_w, ada_b, ln_g, ln_b, gm_w_in, gm_b_in, gm_lnv_g, gm_lnv_b,
              gm_w_s, gm_b_s, gm_w_out, kv_ada_w, kv_ada_b, w_kv, attn_w_q, attn_w_out,
              moe_w_router, moe_b_router, moe_w_gu, moe_b_gu, moe_w_down, moe_b_down):
    c_act = jax.nn.silu(c)
    k_blocks = v_blocks = k_means = None
    for l in range(DEPTH):
        mods = c_act @ ada_w[l] + ada_b[l]
        sh1, sc1, g1, sh2, sc2, g2 = jnp.split(mods, 6, axis=-1)
        h = _modulate(x, sh1, sc1)
        if l < N_A_LAYERS:
            i = l
            h = _gmlp_chunk_mixer(h, gm_w_in[i], gm_b_in[i], gm_lnv_g[i], gm_lnv_b[i],
                                  gm_w_s[i], gm_b_s[i], gm_w_out[i])
        else:
            j = l - N_A_LAYERS
            h = _moba_attention(h, attn_w_q[j], attn_w_out[j], k_blocks, v_blocks, k_means)
        x = _layer_norm(DEEPNORM_ALPHA * x + (1.0 + g1)[:, None, :] * h, ln_g[l, 0], ln_b[l, 0])
        h = _routed_moe(_modulate(x, sh2, sc2), moe_w_router[l], moe_b_router[l],
                        moe_w_gu[l], moe_b_gu[l], moe_w_down[l], moe_b_down[l])
        x = _layer_norm(DEEPNORM_ALPHA * x + (1.0 + g2)[:, None, :] * h, ln_g[l, 1], ln_b[l, 1])
        if l == N_A_LAYERS - 1:
            k_blocks, v_blocks, k_means = _shared_kv(x, c_act, kv_ada_w, kv_ada_b, w_kv)
    return x
```

```python
import functools

import jax
import jax.numpy as jnp
from jax import lax
from jax.experimental import pallas as pl
from jax.experimental.pallas import tpu as pltpu

HEAD_DIM = 128
MOBA_BLOCK = 256
MOBA_TOPK = 3
TOP_K_EXPERTS = 4
SWIGLU_LIMIT = 7.0
SWIGLU_ALPHA = 1.702
LN_EPS = 1e-5

LANES = 128
SUBLANES = 8
VMEM_BYTES_V7X = 64 * 1024 * 1024

ROW_TILE = 256
MOE_TILE = 256
MODS_PAD_ROWS = SUBLANES
MASK_VALUE = -1e30
HI16 = -65536

F32 = jnp.float32
BF16 = jnp.bfloat16
I32 = jnp.int32


def _params(vmem_mb, semantics):
    return pltpu.CompilerParams(dimension_semantics=semantics,
                                vmem_limit_bytes=vmem_mb * 1024 * 1024)


def _resident(shape):
    nd = len(shape)
    return pl.BlockSpec(shape, lambda *_: (0,) * nd, pipeline_mode=pl.Buffered(1))


def _layer_norm(r, g, b):
    mu = jnp.mean(r, axis=-1, keepdims=True)
    d = r - mu
    var = jnp.mean(d * d, axis=-1, keepdims=True)
    return d * lax.rsqrt(var + LN_EPS) * g + b


def _pack_bf16_pair(lo, hi):
    lo = lax.bitcast_convert_type(lo.astype(BF16).astype(F32), I32)
    hi = lax.bitcast_convert_type(hi.astype(BF16).astype(F32), I32)
    return lax.shift_right_logical(lo, 16) | (hi & HI16)


def _unpack_bf16_pair(w):
    lo = lax.bitcast_convert_type(lax.shift_left(w, 16), F32)
    hi = lax.bitcast_convert_type(w & HI16, F32)
    return lo, hi


def _mods_kernel(c_ref, w_ref, b_ref, o_ref):
    c = c_ref[...]
    ca = (c * jax.nn.sigmoid(c)).astype(BF16)
    w = w_ref[0].astype(BF16)
    o_ref[0] = jnp.dot(ca, w, preferred_element_type=F32) + b_ref[0]


def _mods_call(c_pad, w, b, tn=1024):
    L, D, N = w.shape
    return pl.pallas_call(
        _mods_kernel,
        grid=(L, N // tn),
        in_specs=[
            pl.BlockSpec((MODS_PAD_ROWS, D), lambda l, j: (0, 0)),
            pl.BlockSpec((1, D, tn), lambda l, j: (l, 0, j)),
            pl.BlockSpec((1, 1, tn), lambda l, j: (l, 0, j)),
        ],
        out_specs=pl.BlockSpec((1, MODS_PAD_ROWS, tn), lambda l, j: (l, 0, j)),
        out_shape=jax.ShapeDtypeStruct((L, MODS_PAD_ROWS, N), F32),
        compiler_params=_params(40, ("arbitrary", "arbitrary")),
        name="adaln_mods",
    )(c_pad, w, b.reshape(L, 1, N))


def _gmlp_in_kernel(x_ref, sh_ref, sc_ref, w_ref, b_ref, lg_ref, lb_ref, ws_ref, bs_ref,
                    y_ref, u_scr, v_scr, vn_scr, *, ncol):
    tm, D = x_ref.shape
    W = y_ref.shape[1]
    G, C, _ = ws_ref.shape
    gd = W // G
    h = (x_ref[...] * (1.0 + sc_ref[0]) + sh_ref[0]).astype(BF16)
    for j in range(2 * W // ncol):
        z = jnp.dot(h, w_ref[:, j * ncol:(j + 1) * ncol], preferred_element_type=F32)
        z = jax.nn.gelu(z + b_ref[:, j * ncol:(j + 1) * ncol], approximate=True)
        if j * ncol < W:
            u_scr[:, j * ncol:(j + 1) * ncol] = z
        else:
            v_scr[:, j * ncol - W:(j + 1) * ncol - W] = z
    vn_scr[...] = _layer_norm(v_scr[...], lg_ref[...], lb_ref[...]).astype(BF16)
    for n in range(tm // C):
        rows = slice(n * C, (n + 1) * C)
        for g in range(G):
            cols = slice(g * gd, (g + 1) * gd)
            sv = jnp.dot(ws_ref[g], vn_scr[rows, cols], preferred_element_type=F32)
            sv = sv + bs_ref[:, g:g + 1]
            y_ref[rows, cols] = (u_scr[rows, cols] * sv).astype(BF16)


def _gmlp_in_call(x, mods, sh_row, sc_row, w_in, b_in, lnv_g, lnv_b, ws, bs_t, S):
    T, D = x.shape
    W = w_in.shape[1] // 2
    G, C, _ = ws.shape
    tm = ROW_TILE
    tpb = S // tm
    mod = lambda row: pl.BlockSpec((1, 1, D), lambda i: (row(i // tpb), 0, 0))
    return pl.pallas_call(
        functools.partial(_gmlp_in_kernel, ncol=512),
        grid=(T // tm,),
        in_specs=[
            pl.BlockSpec((tm, D), lambda i: (i, 0)),
            mod(sh_row), mod(sc_row),
            _resident((D, 2 * W)),
            _resident((1, 2 * W)),
            _resident((1, W)), _resident((1, W)),
            _resident((G, C, C)),
            _resident((C, G)),
        ],
        out_specs=pl.BlockSpec((tm, W), lambda i: (i, 0)),
        out_shape=jax.ShapeDtypeStruct((T, W), BF16),
        scratch_shapes=[pltpu.VMEM((tm, W), F32), pltpu.VMEM((tm, W), F32),
                        pltpu.VMEM((tm, W), BF16)],
        compiler_params=_params(48, ("arbitrary",)),
        name="gmlp_in",
    )(x, mods, mods, w_in, b_in, lnv_g, lnv_b, ws, bs_t)


def _proj_ln_router_kernel(a_ref, w_ref, x_ref, g1_ref, lg_ref, lb_ref, sh_ref, sc_ref,
                           wr_ref, br_ref,
                           x1_ref, xp_ref, meta_ref, cnt_ref, carry, *, alpha):
    tm, D = x_ref.shape
    E = wr_ref.shape[1]
    i = pl.program_id(0)

    @pl.when(i == 0)
    def _():
        carry[...] = jnp.zeros_like(carry)

    h = jnp.dot(a_ref[...], w_ref[...], preferred_element_type=F32)
    x1 = _layer_norm(alpha * x_ref[...] + (1.0 + g1_ref[0]) * h, lg_ref[0], lb_ref[0])
    x1_ref[...] = x1
    xm = x1 * (1.0 + sc_ref[0]) + sh_ref[0]
    xp_ref[...] = _pack_bf16_pair(xm[:, :D // 2], xm[:, D // 2:])

    logits = jnp.dot(xm.astype(BF16), wr_ref[...], preferred_element_type=F32) + br_ref[...]
    lane = lax.broadcasted_iota(I32, (tm, E), 1).astype(F32)
    onehots, vals, idxs = [], [], []
    cur = logits
    for _k in range(TOP_K_EXPERTS):
        m = jnp.max(cur, axis=-1, keepdims=True)
        idx = jnp.min(jnp.where(cur == m, lane, float(E)), axis=-1, keepdims=True)
        oh = lane == idx
        cur = jnp.where(oh, -jnp.inf, cur)
        onehots.append(jnp.where(oh, 1.0, 0.0))
        vals.append(m)
        idxs.append(idx.astype(I32))
    exps = [jnp.exp(v - vals[0]) for v in vals]
    denom = exps[0]
    for e in exps[1:]:
        denom = denom + e
    oh_all = onehots[0]
    for oh in onehots[1:]:
        oh_all = oh_all + oh
    r_i = lax.broadcasted_iota(I32, (tm, tm), 0)
    c_i = lax.broadcasted_iota(I32, (tm, tm), 1)
    tri = jnp.where(r_i > c_i, 1.0, 0.0).astype(BF16)
    before = jnp.dot(tri, oh_all.astype(BF16), preferred_element_type=F32) + carry[...]
    mlane = lax.broadcasted_iota(I32, (tm, LANES), 1)
    meta = jnp.zeros((tm, LANES), I32)
    for k in range(TOP_K_EXPERTS):
        rank = jnp.sum(onehots[k] * before, axis=-1, keepdims=True).astype(I32)
        wk = lax.bitcast_convert_type(exps[k] / denom, I32)
        meta = jnp.where(mlane == k, idxs[k], meta)
        meta = jnp.where(mlane == TOP_K_EXPERTS + k, wk, meta)
        meta = jnp.where(mlane == 2 * TOP_K_EXPERTS + k, rank, meta)
    meta_ref[...] = meta
    carry[...] = carry[...] + jnp.sum(oh_all, axis=0, keepdims=True)
    cnt_ref[...] = carry[...].astype(I32)


def _proj_ln_router_call(a, w, x, mods, g1_row, sh_row, sc_row, ln, ln_row, w_r, b_r, S, alpha):
    T, D = x.shape
    K = a.shape[1]
    E = w_r.shape[1]
    tm = ROW_TILE
    tpb = S // tm
    mod = lambda row: pl.BlockSpec((1, 1, D), lambda i: (row(i // tpb), 0, 0))
    lnspec = pl.BlockSpec((1, 1, D), lambda i: (ln_row, 0, 0))
    return pl.pallas_call(
        functools.partial(_proj_ln_router_kernel, alpha=alpha),
        grid=(T // tm,),
        in_specs=[
            pl.BlockSpec((tm, K), lambda i: (i, 0)),
            _resident((K, D)),
            pl.BlockSpec((tm, D), lambda i: (i, 0)),
            mod(g1_row), lnspec, pl.BlockSpec((1, 1, D), lambda i: (ln_row, 0, 0)),
            mod(sh_row), mod(sc_row),
            _resident((D, E)), _resident((1, E)),
        ],
        out_specs=[
            pl.BlockSpec((tm, D), lambda i: (i, 0)),
            pl.BlockSpec((tm, D // 2), lambda i: (i, 0)),
            pl.BlockSpec((tm, LANES), lambda i: (i, 0)),
            pl.BlockSpec((1, E), lambda i: (0, 0)),
        ],
        out_shape=[
            jax.ShapeDtypeStruct((T, D), F32),
            jax.ShapeDtypeStruct((T, D // 2), I32),
            jax.ShapeDtypeStruct((T, LANES), I32),
            jax.ShapeDtypeStruct((1, E), I32),
        ],
        scratch_shapes=[pltpu.VMEM((1, E), F32)],
        compiler_params=_params(48, ("arbitrary",)),
        name="proj_ln_router",
    )(a, w, x, mods, ln[0], ln[1], mods, mods, w_r, b_r)


def _moe_kernel(te_ref, nv_ref, inv_ref,
                xp_ref, wgu_ref, bgu_ref, wd_ref, bd_ref,
                y_hbm, g_scr, o_scr, sem, *, n_tokens, spt):
    tm = g_scr.shape[0] // spt
    F = wd_ref.shape[1]
    D = wd_ref.shape[2]
    i = pl.program_id(0)
    nt = pl.num_programs(0)
    nv = nv_ref[0]
    slot = i % 2
    n_pairs = TOP_K_EXPERTS * n_tokens

    def wait_slot(s):
        pltpu.make_async_copy(o_scr.at[s], y_hbm.at[pl.ds(0, tm * spt)], sem.at[s]).wait()

    @pl.when(i == 0)
    def _():
        o_scr[...] = jnp.zeros_like(o_scr)

    @pl.when(jnp.logical_and(i >= 2, i - 2 < nv))
    def _():
        wait_slot(slot)

    @pl.when(i < nv)
    def _():
        base = i * tm
        unroll = 8

        def gather(rb, c):
            for q in range(unroll):
                r = rb * unroll + q
                tok = lax.shift_right_logical(jnp.maximum(inv_ref[base + r], 0), 2)
                src = pl.multiple_of(tok * spt, spt)
                dst = pl.multiple_of(r * spt, spt)
                g_scr[pl.ds(dst, spt), :] = xp_ref[pl.ds(src, spt), :]
            return c

        lax.fori_loop(0, tm // unroll, gather, 0)

        los, his = [], []
        for s in range(spt):
            lo, hi = _unpack_bf16_pair(g_scr[pl.ds(s, tm, stride=spt), :])
            los.append(lo)
            his.append(hi)
        xt = jnp.concatenate(los + his, axis=1).astype(BF16)

        h = jnp.dot(xt, wgu_ref[0], preferred_element_type=F32) + bgu_ref[0]
        g = jnp.minimum(h[:, :F], SWIGLU_LIMIT)
        u = jnp.clip(h[:, F:], -SWIGLU_LIMIT, SWIGLU_LIMIT)
        act = ((u + 1.0) * g * jax.nn.sigmoid(SWIGLU_ALPHA * g)).astype(BF16)
        y = jnp.dot(act, wd_ref[0], preferred_element_type=F32) + bd_ref[0]
        pk = _pack_bf16_pair(y[:, :D // 2], y[:, D // 2:])
        for s in range(spt):
            o_scr[slot, pl.ds(s, tm, stride=spt), :] = pk[:, s * LANES:(s + 1) * LANES]

        def scatter(rb, c):
            for q in range(unroll):
                r = rb * unroll + q
                p = inv_ref[base + r]
                grp = (p & (TOP_K_EXPERTS - 1)) * n_tokens + lax.shift_right_logical(p, 2)
                grp = jnp.where(p < 0, n_pairs + slot * tm + r, grp)
                pltpu.make_async_copy(
                    o_scr.at[slot, pl.ds(pl.multiple_of(r * spt, spt), spt)],
                    y_hbm.at[pl.ds(pl.multiple_of(grp * spt, spt), spt)],
                    sem.at[slot]).start()
            return c

        lax.fori_loop(0, tm // unroll, scatter, 0)

    @pl.when(i == nt - 1)
    def _():
        @pl.when(jnp.logical_and(nt >= 2, nt - 2 < nv))
        def _():
            wait_slot((nt - 2) % 2)

        @pl.when(nt - 1 < nv)
        def _():
            wait_slot((nt - 1) % 2)

        for s in range(2):
            dump = pltpu.make_async_copy(
                o_scr.at[s], y_hbm.at[pl.ds((n_pairs + s * tm) * spt, tm * spt)], sem.at[s])
            dump.start()
            dump.wait()


def _moe_call(tile_expert, n_valid, inv, xp, w_gu, b_gu, w_down, b_down, n_tokens):
    E, D, F2 = w_gu.shape
    F = F2 // 2
    spt = (D // 2) // LANES
    tm = MOE_TILE
    nt = tile_expert.shape[0]
    n_groups = TOP_K_EXPERTS * n_tokens + 2 * tm
    grid_spec = pltpu.PrefetchScalarGridSpec(
        num_scalar_prefetch=3,
        grid=(nt,),
        in_specs=[
            pl.BlockSpec((n_tokens * spt, LANES), lambda i, te, nv, inv: (0, 0),
                         pipeline_mode=pl.Buffered(1)),
            pl.BlockSpec((1, D, F2), lambda i, te, nv, inv: (te[i], 0, 0)),
            pl.BlockSpec((1, 1, F2), lambda i, te, nv, inv: (te[i], 0, 0)),
            pl.BlockSpec((1, F, D), lambda i, te, nv, inv: (te[i], 0, 0)),
            pl.BlockSpec((1, 1, D), lambda i, te, nv, inv: (te[i], 0, 0)),
        ],
        out_specs=pl.BlockSpec(memory_space=pl.ANY),
        scratch_shapes=[
            pltpu.VMEM((tm * spt, LANES), I32),
            pltpu.VMEM((2, tm * spt, LANES), I32),
            pltpu.SemaphoreType.DMA((2,)),
        ],
    )
    return pl.pallas_call(
        functools.partial(_moe_kernel, n_tokens=n_tokens, spt=spt),
        grid_spec=grid_spec,
        out_shape=jax.ShapeDtypeStruct((n_groups * spt, LANES), I32),
        compiler_params=_params(60, ("arbitrary",)),
        name="moe_experts",
    )(tile_expert, n_valid, inv, xp, w_gu, b_gu.reshape(E, 1, F2), w_down,
      b_down.reshape(E, 1, D))


def _moe_routing(meta, counts, n_tokens, n_experts):
    tm = MOE_TILE
    nt = (TOP_K_EXPERTS * n_tokens) // tm + n_experts
    ei = meta[:, 0:TOP_K_EXPERTS]
    rk = meta[:, 2 * TOP_K_EXPERTS:3 * TOP_K_EXPERTS]
    cnt = counts[0]
    tiles = (cnt + tm - 1) // tm
    tile_end = jnp.cumsum(tiles)
    group_start = (tile_end - tiles) * tm
    dest = (group_start[ei] + rk).reshape(-1)
    inv = jnp.full((nt * tm,), -1, I32).at[dest].set(
        jnp.arange(TOP_K_EXPERTS * n_tokens, dtype=I32))
    n_valid = tile_end[-1:].astype(I32)
    tid = jnp.minimum(jnp.arange(nt, dtype=I32), n_valid[0] - 1)
    tile_expert = jnp.searchsorted(tile_end, tid, side="right").astype(I32)
    return tile_expert, n_valid, inv


def _combine_ln_kernel(*refs, alpha, spt):
    y_refs = refs[:TOP_K_EXPERTS]
    meta_ref, x1_ref, g2_ref, lg_ref, lb_ref, o_ref = refs[TOP_K_EXPERTS:]
    tc = x1_ref.shape[0]
    los = [None] * spt
    his = [None] * spt
    for k in range(TOP_K_EXPERTS):
        wk = lax.bitcast_convert_type(
            meta_ref[:, TOP_K_EXPERTS + k:TOP_K_EXPERTS + k + 1], F32)
        for s in range(spt):
            lo, hi = _unpack_bf16_pair(y_refs[k][pl.ds(s, tc, stride=spt), :])
            los[s] = wk * lo if k == 0 else los[s] + wk * lo
            his[s] = wk * hi if k == 0 else his[s] + wk * hi
    moe = jnp.concatenate(los + his, axis=1)
    r = alpha * x1_ref[...] + (1.0 + g2_ref[0]) * moe
    o_ref[...] = _layer_norm(r, lg_ref[0], lb_ref[0])


def _combine_ln_call(y, meta, x1, mods, g2_row, ln, ln_row, S, alpha):
    T, D = x1.shape
    spt = (D // 2) // LANES
    tc = ROW_TILE
    tpb = S // tc
    nblk = T // tc
    y_spec = lambda k: pl.BlockSpec((tc * spt, LANES), lambda i: (k * nblk + i, 0))
    return pl.pallas_call(
        functools.partial(_combine_ln_kernel, alpha=alpha, spt=spt),
        grid=(nblk,),
        in_specs=[y_spec(k) for k in range(TOP_K_EXPERTS)] + [
            pl.BlockSpec((tc, LANES), lambda i: (i, 0)),
            pl.BlockSpec((tc, D), lambda i: (i, 0)),
            pl.BlockSpec((1, 1, D), lambda i: (g2_row(i // tpb), 0, 0)),
            pl.BlockSpec((1, 1, D), lambda i: (ln_row, 0, 0)),
            pl.BlockSpec((1, 1, D), lambda i: (ln_row, 0, 0)),
        ],
        out_specs=pl.BlockSpec((tc, D), lambda i: (i, 0)),
        out_shape=jax.ShapeDtypeStruct((T, D), F32),
        compiler_params=_params(40, ("arbitrary",)),
        name="moe_combine_ln",
    )(*([y] * TOP_K_EXPERTS), meta, x1, mods, ln[0], ln[1])


def _mod_matmul_kernel(x_ref, sh_ref, sc_ref, w_ref, o_ref, *rest, out_scale, mean_cols):
    h = (x_ref[...] * (1.0 + sc_ref[0]) + sh_ref[0]).astype(BF16)
    z = jnp.dot(h, w_ref[...], preferred_element_type=F32)
    if out_scale != 1.0:
        z = z * out_scale
    o_ref[...] = z.astype(BF16)
    if mean_cols:
        rest[0][0] = jnp.mean(z[:, :mean_cols], axis=0, keepdims=True)


def _mod_matmul_call(x, mods, sh_row, sc_row, w, S, out_scale=1.0, mean_cols=0):
    T, D = x.shape
    N = w.shape[1]
    tm = MOBA_BLOCK
    tpb = S // tm
    mod = lambda row: pl.BlockSpec((1, 1, D), lambda i: (row(i // tpb), 0, 0))
    out_specs = [pl.BlockSpec((tm, N), lambda i: (i, 0))]
    out_shape = [jax.ShapeDtypeStruct((T, N), BF16)]
    if mean_cols:
        out_specs.append(pl.BlockSpec((1, 1, mean_cols), lambda i: (i, 0, 0)))
        out_shape.append(jax.ShapeDtypeStruct((T // tm, 1, mean_cols), F32))
    outs = pl.pallas_call(
        functools.partial(_mod_matmul_kernel, out_scale=out_scale, mean_cols=mean_cols),
        grid=(T // tm,),
        in_specs=[pl.BlockSpec((tm, D), lambda i: (i, 0)), mod(sh_row), mod(sc_row),
                  _resident((D, N))],
        out_specs=out_specs,
        out_shape=out_shape,
        compiler_params=_params(48, ("arbitrary",)),
        name="mod_matmul",
    )(x, mods, mods, w)
    return tuple(outs) if mean_cols else outs[0]


def _moba_kernel(q_ref, k_ref, v_ref, km_ref, o_ref, m_scr, l_scr, acc_scr):
    bs = q_ref.shape[0]
    nb = km_ref.shape[1]
    qb = pl.program_id(2)
    q = q_ref[...]
    dn = (((1,), (1,)), ((), ()))

    km = km_ref[0]
    km_hi = km.astype(BF16)
    km_lo = (km - km_hi.astype(F32)).astype(BF16)
    gate = (lax.dot_general(q, km_hi, dn, preferred_element_type=F32)
            + lax.dot_general(q, km_lo, dn, preferred_element_type=F32))
    blk = lax.broadcasted_iota(I32, (bs, nb), 1)
    valid = blk < qb
    gate = jnp.where(valid, gate, -jnp.inf)
    beaten = jnp.zeros((bs, nb), F32)
    for jp in range(nb):
        col = gate[:, jp:jp + 1]
        wins = jnp.logical_or(col > gate, jnp.logical_and(col == gate, jp < blk))
        beaten = beaten + jnp.where(wins, 1.0, 0.0)
    sel = jnp.where(jnp.logical_and(valid, beaten < MOBA_TOPK), 1.0, 0.0)

    start = pl.multiple_of(qb * bs, bs)
    s = lax.dot_general(q, k_ref[pl.ds(start, bs), :], dn, preferred_element_type=F32)
    row = lax.broadcasted_iota(I32, (bs, bs), 0)
    colk = lax.broadcasted_iota(I32, (bs, bs), 1)
    s = jnp.where(colk <= row, s, MASK_VALUE)
    m = jnp.max(s, axis=-1, keepdims=True)
    p = jnp.exp(s - m)
    m_scr[...] = m
    l_scr[...] = jnp.sum(p, axis=-1, keepdims=True)
    acc_scr[...] = jnp.dot(p.astype(BF16), v_ref[pl.ds(start, bs), :],
                           preferred_element_type=F32)

    for j in range(nb - 1):
        @pl.when(j < qb)
        def _(j=j):
            sj = lax.dot_general(q, k_ref[j * bs:(j + 1) * bs, :], dn,
                                 preferred_element_type=F32)
            sj = jnp.where(sel[:, j:j + 1] > 0.0, sj, MASK_VALUE)
            m_old = m_scr[...]
            m_new = jnp.maximum(m_old, jnp.max(sj, axis=-1, keepdims=True))
            a = jnp.exp(m_old - m_new)
            pj = jnp.exp(sj - m_new)
            l_scr[...] = a * l_scr[...] + jnp.sum(pj, axis=-1, keepdims=True)
            acc_scr[...] = a * acc_scr[...] + jnp.dot(
                pj.astype(BF16), v_ref[j * bs:(j + 1) * bs, :], preferred_element_type=F32)
            m_scr[...] = m_new

    o_ref[...] = (acc_scr[...] / l_scr[...]).astype(BF16)


def _moba_call(q, kv, kmeans, B, S):
    T, HD = q.shape
    H = HD // HEAD_DIM
    bs = MOBA_BLOCK
    nb = S // bs
    return pl.pallas_call(
        _moba_kernel,
        grid=(B, H, nb),
        in_specs=[
            pl.BlockSpec((bs, HEAD_DIM), lambda b, h, i: (b * nb + i, h)),
            pl.BlockSpec((S, HEAD_DIM), lambda b, h, i: (b, h)),
            pl.BlockSpec((S, HEAD_DIM), lambda b, h, i: (b, H + h)),
            pl.BlockSpec((1, nb, HEAD_DIM), lambda b, h, i: (b, 0, h)),
        ],
        out_specs=pl.BlockSpec((bs, HEAD_DIM), lambda b, h, i: (b * nb + i, h)),
        out_shape=jax.ShapeDtypeStruct((T, HD), BF16),
        scratch_shapes=[pltpu.VMEM((bs, 1), F32), pltpu.VMEM((bs, 1), F32),
                        pltpu.VMEM((bs, HEAD_DIM), F32)],
        compiler_params=_params(32, ("arbitrary", "arbitrary", "arbitrary")),
        name="moba_attention",
    )(q, kv, kv, kmeans.reshape(B, nb, HD))


def kernel(x, c, ada_w, ada_b, ln_g, ln_b, gm_w_in, gm_b_in, gm_lnv_g, gm_lnv_b, gm_w_s,
           gm_b_s, gm_w_out, kv_ada_w, kv_ada_b, w_kv, attn_w_q, attn_w_out, moe_w_router,
           moe_b_router, moe_w_gu, moe_b_gu, moe_w_down, moe_b_down):
    B, S, D = x.shape
    T = B * S
    depth = ada_w.shape[0]
    n_a = gm_w_in.shape[0]
    E = moe_w_router.shape[2]
    W = gm_w_in.shape[2] // 2
    C = gm_w_s.shape[2]
    assert B <= MODS_PAD_ROWS and S % MOBA_BLOCK == 0 and S % ROW_TILE == 0
    assert ROW_TILE % C == 0 and W // gm_w_s.shape[1] == LANES
    assert (D // 2) % LANES == 0 and (TOP_K_EXPERTS * T) % MOE_TILE == 0
    alpha = (2.0 * depth) ** 0.25

    c_pad = jnp.zeros((MODS_PAD_ROWS, D), F32).at[:B].set(c)
    n_mod = ada_w.shape[2] // D
    mods = _mods_call(c_pad, ada_w, ada_b).reshape(depth * MODS_PAD_ROWS * n_mod, 1, D)
    kv_mods = _mods_call(c_pad, kv_ada_w[None], kv_ada_b[None]).reshape(
        MODS_PAD_ROWS * 2, 1, D)

    def mod_row(l, which):
        return lambda b: (l * MODS_PAD_ROWS + b) * n_mod + which

    ln_g3 = ln_g.reshape(depth * 2, 1, D)
    ln_b3 = ln_b.reshape(depth * 2, 1, D)
    causal = jnp.tril(jnp.ones((C, C), F32))
    xt = x.reshape(T, D)
    kv = kmeans = None
    for l in range(depth):
        if l < n_a:
            y = _gmlp_in_call(
                xt, mods, mod_row(l, 0), mod_row(l, 1), gm_w_in[l].astype(BF16),
                gm_b_in[l][None], gm_lnv_g[l][None], gm_lnv_b[l][None],
                (gm_w_s[l] * causal[None]).astype(BF16), gm_b_s[l].T, S)
            w_o = gm_w_out[l].astype(BF16)
        else:
            j = l - n_a
            q = _mod_matmul_call(xt, mods, mod_row(l, 0), mod_row(l, 1),
                                 attn_w_q[j].astype(BF16), S, out_scale=HEAD_DIM ** -0.5)
            y = _moba_call(q, kv, kmeans, B, S)
            w_o = attn_w_out[j].astype(BF16)
        x1, xp, meta, counts = _proj_ln_router_call(
            y, w_o, xt, mods, mod_row(l, 2), mod_row(l, 3), mod_row(l, 4),
            (ln_g3, ln_b3), 2 * l, moe_w_router[l].astype(BF16), moe_b_router[l][None],
            S, alpha)
        tile_expert, n_valid, inv = _moe_routing(meta, counts, T, E)
        spt = (D // 2) // LANES
        ypairs = _moe_call(tile_expert, n_valid, inv, xp.reshape(T * spt, LANES),
                           moe_w_gu[l].astype(BF16), moe_b_gu[l],
                           moe_w_down[l].astype(BF16), moe_b_down[l], T)
        xt = _combine_ln_call(ypairs, meta, x1, mods, mod_row(l, 5), (ln_g3, ln_b3),
                              2 * l + 1, S, alpha)
        if l == n_a - 1:
            kv, kmeans = _mod_matmul_call(
                xt, kv_mods, lambda b: b * 2, lambda b: b * 2 + 1, w_kv.astype(BF16), S,
                mean_cols=w_kv.shape[1] // 2)
    return xt.reshape(B, S, D)
```

```python
import functools

import jax
import jax.numpy as jnp
from jax import lax
from jax.experimental import pallas as pl
from jax.experimental.pallas import tpu as pltpu

HEAD_DIM = 128
MOBA_BLOCK = 256
MOBA_TOPK = 3
TOP_K_EXPERTS = 4
SWIGLU_LIMIT = 7.0
SWIGLU_ALPHA = 1.702
LN_EPS = 1e-5

LANES = 128
SUBLANES = 8

ROW_TILE = 256
MOE_TILE = 256
MODS_PAD_ROWS = SUBLANES
MASK_VALUE = -1e30
LOG2_E = 1.4426950408889634
HI16 = -65536

F32 = jnp.float32
BF16 = jnp.bfloat16
I32 = jnp.int32


def _params(vmem_mb, semantics):
    return pltpu.CompilerParams(dimension_semantics=semantics,
                                vmem_limit_bytes=vmem_mb * 1024 * 1024)


def _resident(shape):
    nd = len(shape)
    return pl.BlockSpec(shape, lambda *_: (0,) * nd, pipeline_mode=pl.Buffered(1))


def _layer_resident(tail, l):
    nd = len(tail)
    return pl.BlockSpec((1,) + tuple(tail), lambda *_: (l,) + (0,) * nd,
                        pipeline_mode=pl.Buffered(1))


def _layer_norm(r, g, b):
    mu = jnp.mean(r, axis=-1, keepdims=True)
    d = r - mu
    var = jnp.mean(d * d, axis=-1, keepdims=True)
    return d * lax.rsqrt(var + LN_EPS) * g + b


def _pack_bf16_pair(lo, hi):
    lo = lax.bitcast_convert_type(lo.astype(BF16).astype(F32), I32)
    hi = lax.bitcast_convert_type(hi.astype(BF16).astype(F32), I32)
    return lax.shift_right_logical(lo, 16) | (hi & HI16)


def _unpack_bf16_pair(w):
    lo = lax.bitcast_convert_type(lax.shift_left(w, 16), F32)
    hi = lax.bitcast_convert_type(w & HI16, F32)
    return lo, hi


def _mods_kernel(c_ref, w_ref, b_ref, o_ref):
    c = c_ref[...]
    ca = (c * jax.nn.sigmoid(c)).astype(BF16)
    w = w_ref[0].astype(BF16)
    o_ref[0] = jnp.dot(ca, w, preferred_element_type=F32) + b_ref[0]


def _mods_call(c_pad, w, b, tn=1024):
    L, D, N = w.shape
    return pl.pallas_call(
        _mods_kernel,
        grid=(L, N // tn),
        in_specs=[
            pl.BlockSpec((MODS_PAD_ROWS, D), lambda l, j: (0, 0)),
            pl.BlockSpec((1, D, tn), lambda l, j: (l, 0, j)),
            pl.BlockSpec((1, 1, tn), lambda l, j: (l, 0, j)),
        ],
        out_specs=pl.BlockSpec((1, MODS_PAD_ROWS, tn), lambda l, j: (l, 0, j)),
        out_shape=jax.ShapeDtypeStruct((L, MODS_PAD_ROWS, N), F32),
        compiler_params=_params(40, ("arbitrary", "arbitrary")),
        name="adaln_mods",
    )(c_pad, w, b.reshape(L, 1, N))


def _gmlp_in_kernel(x_ref, sh_ref, sc_ref, w_ref, b_ref, lg_ref, lb_ref, ws_ref, bs_ref,
                    y_ref, u_scr, v_scr, vn_scr, *, ncol):
    tm, D = x_ref.shape
    W = y_ref.shape[1]
    _, G, C, _ = ws_ref.shape
    gd = W // G
    h = (x_ref[...] * (1.0 + sc_ref[0]) + sh_ref[0]).astype(BF16)
    for j in range(2 * W // ncol):
        z = jnp.dot(h, w_ref[0, :, j * ncol:(j + 1) * ncol], preferred_element_type=F32)
        z = jax.nn.gelu(z + b_ref[0, :, j * ncol:(j + 1) * ncol], approximate=True)
        if j * ncol < W:
            u_scr[:, j * ncol:(j + 1) * ncol] = z
        else:
            v_scr[:, j * ncol - W:(j + 1) * ncol - W] = z
    vn_scr[...] = _layer_norm(v_scr[...], lg_ref[0], lb_ref[0]).astype(BF16)
    for n in range(tm // C):
        rows = slice(n * C, (n + 1) * C)
        for g in range(G):
            cols = slice(g * gd, (g + 1) * gd)
            sv = jnp.dot(ws_ref[0, g], vn_scr[rows, cols], preferred_element_type=F32)
            sv = sv + bs_ref[0, :, g:g + 1]
            y_ref[rows, cols] = (u_scr[rows, cols] * sv).astype(BF16)


def _gmlp_in_call(x, mods, sh_row, sc_row, w_in, b_in, lnv_g, lnv_b, ws, bs_t, l, S):
    T, D = x.shape
    W = w_in.shape[2] // 2
    _, G, C, _ = ws.shape
    tm = ROW_TILE
    tpb = S // tm
    mod = lambda row: pl.BlockSpec((1, 1, D), lambda i: (row(i // tpb), 0, 0))
    return pl.pallas_call(
        functools.partial(_gmlp_in_kernel, ncol=512),
        grid=(T // tm,),
        in_specs=[
            pl.BlockSpec((tm, D), lambda i: (i, 0)),
            mod(sh_row), mod(sc_row),
            _layer_resident((D, 2 * W), l),
            _layer_resident((1, 2 * W), l),
            _layer_resident((1, W), l), _layer_resident((1, W), l),
            _layer_resident((G, C, C), l),
            _layer_resident((C, G), l),
        ],
        out_specs=pl.BlockSpec((tm, W), lambda i: (i, 0)),
        out_shape=jax.ShapeDtypeStruct((T, W), BF16),
        scratch_shapes=[pltpu.VMEM((tm, W), F32), pltpu.VMEM((tm, W), F32),
                        pltpu.VMEM((tm, W), BF16)],
        compiler_params=_params(48, ("arbitrary",)),
        name="gmlp_in",
    )(x, mods, mods, w_in, b_in, lnv_g, lnv_b, ws, bs_t)


def _proj_ln_router_kernel(a_ref, w_ref, x_ref, g1_ref, lg_ref, lb_ref, sh_ref, sc_ref,
                           wr_ref, br_ref,
                           x1_ref, xp_ref, meta_ref, cnt_ref, carry, *, alpha):
    tm, D = x_ref.shape
    E = wr_ref.shape[2]
    spt = xp_ref.shape[0] // tm
    i = pl.program_id(0)

    @pl.when(i == 0)
    def _():
        carry[...] = jnp.zeros_like(carry)

    h = jnp.dot(a_ref[...], w_ref[0], preferred_element_type=F32)
    x1 = _layer_norm(alpha * x_ref[...] + (1.0 + g1_ref[0]) * h, lg_ref[0], lb_ref[0])
    x1_ref[...] = x1
    xm = x1 * (1.0 + sc_ref[0]) + sh_ref[0]
    pk = _pack_bf16_pair(xm[:, :D // 2], xm[:, D // 2:])
    for s in range(spt):
        xp_ref[pl.ds(s, tm, stride=spt), :] = pk[:, s * LANES:(s + 1) * LANES]

    logits = jnp.dot(xm.astype(BF16), wr_ref[0], preferred_element_type=F32) + br_ref[0]
    lane = lax.broadcasted_iota(I32, (tm, E), 1).astype(F32)
    onehots, vals, idxs = [], [], []
    cur = logits
    for _k in range(TOP_K_EXPERTS):
        m = jnp.max(cur, axis=-1, keepdims=True)
        idx = jnp.min(jnp.where(cur == m, lane, float(E)), axis=-1, keepdims=True)
        oh = lane == idx
        cur = jnp.where(oh, -jnp.inf, cur)
        onehots.append(jnp.where(oh, 1.0, 0.0))
        vals.append(m)
        idxs.append(idx.astype(I32))
    exps = [jnp.exp(v - vals[0]) for v in vals]
    denom = exps[0]
    for e in exps[1:]:
        denom = denom + e
    oh_all = onehots[0]
    for oh in onehots[1:]:
        oh_all = oh_all + oh
    r_i = lax.broadcasted_iota(I32, (tm, tm), 0)
    c_i = lax.broadcasted_iota(I32, (tm, tm), 1)
    tri = jnp.where(r_i > c_i, 1.0, 0.0).astype(BF16)
    before = jnp.dot(tri, oh_all.astype(BF16), preferred_element_type=F32) + carry[...]
    mlane = lax.broadcasted_iota(I32, (tm, LANES), 1)
    meta = jnp.zeros((tm, LANES), I32)
    for k in range(TOP_K_EXPERTS):
        rank = jnp.sum(onehots[k] * before, axis=-1, keepdims=True).astype(I32)
        wk = lax.bitcast_convert_type(exps[k] / denom, I32)
        meta = jnp.where(mlane == k, idxs[k], meta)
        meta = jnp.where(mlane == TOP_K_EXPERTS + k, wk, meta)
        meta = jnp.where(mlane == 2 * TOP_K_EXPERTS + k, rank, meta)
    meta_ref[...] = meta
    carry[...] = carry[...] + jnp.sum(oh_all, axis=0, keepdims=True)
    cnt_ref[...] = carry[...].astype(I32)


def _proj_ln_router_call(a, w, lw, x, mods, g1_row, sh_row, sc_row, ln, ln_row, w_r, b_r, l,
                         S, alpha):
    T, D = x.shape
    K = a.shape[1]
    E = w_r.shape[2]
    spt = (D // 2) // LANES
    tm = ROW_TILE
    tpb = S // tm
    mod = lambda row: pl.BlockSpec((1, 1, D), lambda i: (row(i // tpb), 0, 0))
    lnspec = pl.BlockSpec((1, 1, D), lambda i: (ln_row, 0, 0))
    return pl.pallas_call(
        functools.partial(_proj_ln_router_kernel, alpha=alpha),
        grid=(T // tm,),
        in_specs=[
            pl.BlockSpec((tm, K), lambda i: (i, 0)),
            _layer_resident((K, D), lw),
            pl.BlockSpec((tm, D), lambda i: (i, 0)),
            mod(g1_row), lnspec, lnspec,
            mod(sh_row), mod(sc_row),
            _layer_resident((D, E), l), _layer_resident((1, E), l),
        ],
        out_specs=[
            pl.BlockSpec((tm, D), lambda i: (i, 0)),
            pl.BlockSpec((tm * spt, LANES), lambda i: (i, 0)),
            pl.BlockSpec((tm, LANES), lambda i: (i, 0)),
            pl.BlockSpec((1, E), lambda i: (0, 0)),
        ],
        out_shape=[
            jax.ShapeDtypeStruct((T, D), F32),
            jax.ShapeDtypeStruct((T * spt, LANES), I32),
            jax.ShapeDtypeStruct((T, LANES), I32),
            jax.ShapeDtypeStruct((1, E), I32),
        ],
        scratch_shapes=[pltpu.VMEM((1, E), F32)],
        compiler_params=_params(48, ("arbitrary",)),
        name="proj_ln_router",
    )(a, w, x, mods, ln[0], ln[1], mods, mods, w_r, b_r)


def _moe_kernel(te_ref, nv_ref, inv_ref,
                xp_ref, wgu_ref, bgu_ref, wd_ref, bd_ref,
                y_hbm, g_scr, o_scr, sem, *, n_tokens, spt):
    tm = g_scr.shape[0] // spt
    F = wd_ref.shape[2]
    D = wd_ref.shape[3]
    i = pl.program_id(0)
    nt = pl.num_programs(0)
    nv = nv_ref[0]
    slot = i % 2
    n_pairs = TOP_K_EXPERTS * n_tokens

    def wait_slot(s):
        pltpu.make_async_copy(o_scr.at[s], y_hbm.at[pl.ds(0, tm * spt)], sem.at[s]).wait()

    @pl.when(i == 0)
    def _():
        o_scr[...] = jnp.zeros_like(o_scr)

    @pl.when(jnp.logical_and(i >= 2, i - 2 < nv))
    def _():
        wait_slot(slot)

    @pl.when(i < nv)
    def _():
        base = i * tm
        unroll = 8

        def gather(rb, c):
            for q in range(unroll):
                r = rb * unroll + q
                tok = lax.shift_right_logical(jnp.maximum(inv_ref[base + r], 0), 2)
                src = pl.multiple_of(tok * spt, spt)
                dst = pl.multiple_of(r * spt, spt)
                g_scr[pl.ds(dst, spt), :] = xp_ref[pl.ds(src, spt), :]
            return c

        lax.fori_loop(0, tm // unroll, gather, 0)

        los, his = [], []
        for s in range(spt):
            lo, hi = _unpack_bf16_pair(g_scr[pl.ds(s, tm, stride=spt), :])
            los.append(lo)
            his.append(hi)
        xt = jnp.concatenate(los + his, axis=1).astype(BF16)

        h = jnp.dot(xt, wgu_ref[0, 0], preferred_element_type=F32) + bgu_ref[0, 0]
        g = jnp.minimum(h[:, :F], SWIGLU_LIMIT)
        u = jnp.clip(h[:, F:], -SWIGLU_LIMIT, SWIGLU_LIMIT)
        act = ((u + 1.0) * g * jax.nn.sigmoid(SWIGLU_ALPHA * g)).astype(BF16)
        y = jnp.dot(act, wd_ref[0, 0], preferred_element_type=F32) + bd_ref[0, 0]
        pk = _pack_bf16_pair(y[:, :D // 2], y[:, D // 2:])
        for s in range(spt):
            o_scr[slot, pl.ds(s, tm, stride=spt), :] = pk[:, s * LANES:(s + 1) * LANES]

        def scatter(rb, c):
            for q in range(unroll):
                r = rb * unroll + q
                p = inv_ref[base + r]
                grp = (p & (TOP_K_EXPERTS - 1)) * n_tokens + lax.shift_right_logical(p, 2)
                grp = jnp.where(p < 0, n_pairs + slot * tm + r, grp)
                pltpu.make_async_copy(
                    o_scr.at[slot, pl.ds(pl.multiple_of(r * spt, spt), spt)],
                    y_hbm.at[pl.ds(pl.multiple_of(grp * spt, spt), spt)],
                    sem.at[slot]).start()
            return c

        lax.fori_loop(0, tm // unroll, scatter, 0)

    @pl.when(i == nt - 1)
    def _():
        @pl.when(jnp.logical_and(nt >= 2, nt - 2 < nv))
        def _():
            wait_slot((nt - 2) % 2)

        @pl.when(nt - 1 < nv)
        def _():
            wait_slot((nt - 1) % 2)

        for s in range(2):
            dump = pltpu.make_async_copy(
                o_scr.at[s], y_hbm.at[pl.ds((n_pairs + s * tm) * spt, tm * spt)], sem.at[s])
            dump.start()
            dump.wait()


def _moe_call(tile_expert, n_valid, inv, xp, w_gu, b_gu, w_down, b_down, l, n_tokens):
    L, E, D, F2 = w_gu.shape
    F = F2 // 2
    spt = (D // 2) // LANES
    tm = MOE_TILE
    nt = tile_expert.shape[0]
    n_groups = TOP_K_EXPERTS * n_tokens + 2 * tm
    grid_spec = pltpu.PrefetchScalarGridSpec(
        num_scalar_prefetch=3,
        grid=(nt,),
        in_specs=[
            pl.BlockSpec((n_tokens * spt, LANES), lambda i, te, nv, inv: (0, 0),
                         pipeline_mode=pl.Buffered(1)),
            pl.BlockSpec((1, 1, D, F2), lambda i, te, nv, inv: (l, te[i], 0, 0)),
            pl.BlockSpec((1, 1, 1, F2), lambda i, te, nv, inv: (l, te[i], 0, 0)),
            pl.BlockSpec((1, 1, F, D), lambda i, te, nv, inv: (l, te[i], 0, 0)),
            pl.BlockSpec((1, 1, 1, D), lambda i, te, nv, inv: (l, te[i], 0, 0)),
        ],
        out_specs=pl.BlockSpec(memory_space=pl.ANY),
        scratch_shapes=[
            pltpu.VMEM((tm * spt, LANES), I32),
            pltpu.VMEM((2, tm * spt, LANES), I32),
            pltpu.SemaphoreType.DMA((2,)),
        ],
    )
    return pl.pallas_call(
        functools.partial(_moe_kernel, n_tokens=n_tokens, spt=spt),
        grid_spec=grid_spec,
        out_shape=jax.ShapeDtypeStruct((n_groups * spt, LANES), I32),
        compiler_params=_params(60, ("arbitrary",)),
        name="moe_experts",
    )(tile_expert, n_valid, inv, xp, w_gu, b_gu.reshape(L, E, 1, F2), w_down,
      b_down.reshape(L, E, 1, D))


def _moe_routing(meta, counts, n_tokens, n_experts):
    tm = MOE_TILE
    nt = (TOP_K_EXPERTS * n_tokens) // tm + n_experts
    ei = meta[:, 0:TOP_K_EXPERTS]
    rk = meta[:, 2 * TOP_K_EXPERTS:3 * TOP_K_EXPERTS]
    cnt = counts[0]
    tiles = (cnt + tm - 1) // tm
    tile_end = jnp.cumsum(tiles)
    group_start = (tile_end - tiles) * tm
    dest = (group_start[ei] + rk).reshape(-1)
    inv = jnp.full((nt * tm,), -1, I32).at[dest].set(
        jnp.arange(TOP_K_EXPERTS * n_tokens, dtype=I32))
    n_valid = tile_end[-1:].astype(I32)
    tid = jnp.minimum(jnp.arange(nt, dtype=I32), n_valid[0] - 1)
    tile_expert = jnp.sum((tile_end[None, :] <= tid[:, None]).astype(I32), axis=1)
    return tile_expert, n_valid, inv


def _combine_ln_kernel(*refs, alpha, spt):
    y_refs = refs[:TOP_K_EXPERTS]
    meta_ref, x1_ref, g2_ref, lg_ref, lb_ref, o_ref = refs[TOP_K_EXPERTS:]
    tc = x1_ref.shape[0]
    los = [None] * spt
    his = [None] * spt
    for k in range(TOP_K_EXPERTS):
        wk = lax.bitcast_convert_type(
            meta_ref[:, TOP_K_EXPERTS + k:TOP_K_EXPERTS + k + 1], F32)
        for s in range(spt):
            lo, hi = _unpack_bf16_pair(y_refs[k][pl.ds(s, tc, stride=spt), :])
            los[s] = wk * lo if k == 0 else los[s] + wk * lo
            his[s] = wk * hi if k == 0 else his[s] + wk * hi
    moe = jnp.concatenate(los + his, axis=1)
    r = alpha * x1_ref[...] + (1.0 + g2_ref[0]) * moe
    o_ref[...] = _layer_norm(r, lg_ref[0], lb_ref[0])


def _combine_ln_call(y, meta, x1, mods, g2_row, ln, ln_row, S, alpha):
    T, D = x1.shape
    spt = (D // 2) // LANES
    tc = ROW_TILE
    tpb = S // tc
    nblk = T // tc
    y_spec = lambda k: pl.BlockSpec((tc * spt, LANES), lambda i: (k * nblk + i, 0))
    return pl.pallas_call(
        functools.partial(_combine_ln_kernel, alpha=alpha, spt=spt),
        grid=(nblk,),
        in_specs=[y_spec(k) for k in range(TOP_K_EXPERTS)] + [
            pl.BlockSpec((tc, LANES), lambda i: (i, 0)),
            pl.BlockSpec((tc, D), lambda i: (i, 0)),
            pl.BlockSpec((1, 1, D), lambda i: (g2_row(i // tpb), 0, 0)),
            pl.BlockSpec((1, 1, D), lambda i: (ln_row, 0, 0)),
            pl.BlockSpec((1, 1, D), lambda i: (ln_row, 0, 0)),
        ],
        out_specs=pl.BlockSpec((tc, D), lambda i: (i, 0)),
        out_shape=jax.ShapeDtypeStruct((T, D), F32),
        compiler_params=_params(40, ("arbitrary",)),
        name="moe_combine_ln",
    )(*([y] * TOP_K_EXPERTS), meta, x1, mods, ln[0], ln[1])


def _q_proj_kernel(x_ref, sh_ref, sc_ref, wt_ref, o_ref, *, out_scale):
    h = (x_ref[...] * (1.0 + sc_ref[0]) + sh_ref[0]).astype(BF16)
    zt = lax.dot_general(wt_ref[0], h, (((1,), (1,)), ((), ())), preferred_element_type=F32)
    o_ref[0] = (zt * out_scale).astype(BF16)


def _q_proj_call(x, mods, sh_row, sc_row, w_t, l, S, out_scale):
    T, D = x.shape
    N = w_t.shape[1]
    tm = MOBA_BLOCK
    tpb = S // tm
    mod = lambda row: pl.BlockSpec((1, 1, D), lambda i: (row(i // tpb), 0, 0))
    return pl.pallas_call(
        functools.partial(_q_proj_kernel, out_scale=out_scale),
        grid=(T // tm,),
        in_specs=[pl.BlockSpec((tm, D), lambda i: (i, 0)), mod(sh_row), mod(sc_row),
                  _layer_resident((N, D), l)],
        out_specs=pl.BlockSpec((1, N, tm), lambda i: (i, 0, 0)),
        out_shape=jax.ShapeDtypeStruct((T // tm, N, tm), BF16),
        compiler_params=_params(48, ("arbitrary",)),
        name="q_proj",
    )(x, mods, mods, w_t)


def _kv_proj_kernel(x_ref, sh_ref, sc_ref, wk_ref, wvt_ref, k_ref, vt_ref, km_ref):
    h = (x_ref[...] * (1.0 + sc_ref[0]) + sh_ref[0]).astype(BF16)
    zk = jnp.dot(h, wk_ref[...], preferred_element_type=F32)
    k_ref[...] = zk.astype(BF16)
    km_ref[0] = jnp.mean(zk, axis=0, keepdims=True)
    zvt = lax.dot_general(wvt_ref[...], h, (((1,), (1,)), ((), ())),
                          preferred_element_type=F32)
    vt_ref[0] = zvt.astype(BF16)


def _kv_proj_call(x, mods, sh_row, sc_row, w_k, w_vt, S):
    T, D = x.shape
    HD = w_k.shape[1]
    tm = MOBA_BLOCK
    tpb = S // tm
    mod = lambda row: pl.BlockSpec((1, 1, D), lambda i: (row(i // tpb), 0, 0))
    return pl.pallas_call(
        _kv_proj_kernel,
        grid=(T // tm,),
        in_specs=[pl.BlockSpec((tm, D), lambda i: (i, 0)), mod(sh_row), mod(sc_row),
                  _resident((D, HD)), _resident((HD, D))],
        out_specs=[pl.BlockSpec((tm, HD), lambda i: (i, 0)),
                   pl.BlockSpec((1, HD, tm), lambda i: (i, 0, 0)),
                   pl.BlockSpec((1, 1, HD), lambda i: (i, 0, 0))],
        out_shape=[jax.ShapeDtypeStruct((T, HD), BF16),
                   jax.ShapeDtypeStruct((T // tm, HD, tm), BF16),
                   jax.ShapeDtypeStruct((T // tm, 1, HD), F32)],
        compiler_params=_params(48, ("arbitrary",)),
        name="kv_proj",
    )(x, mods, mods, w_k, w_vt)


def _moba_kernel(qt_ref, k_ref, vt_ref, km_ref, o_ref, m_scr, l_scr, acc_scr, s_scr, *, hps):
    bs = qt_ref.shape[2]
    nb = km_ref.shape[1]
    qb = pl.program_id(2)
    start = pl.multiple_of(qb * bs, bs)
    kidx = lax.broadcasted_iota(I32, (bs, bs), 0)
    qidx = lax.broadcasted_iota(I32, (bs, bs), 1)
    blk = lax.broadcasted_iota(I32, (nb, bs), 0)
    valid = blk < qb
    heads = [slice(h * HEAD_DIM, (h + 1) * HEAD_DIM) for h in range(hps)]

    biases = []
    for cs in heads:
        qt = qt_ref[0, cs, :]
        km = km_ref[0, :, cs]
        km_hi = km.astype(BF16)
        km_lo = (km - km_hi.astype(F32)).astype(BF16)
        gate = (jnp.dot(km_hi, qt, preferred_element_type=F32)
                + jnp.dot(km_lo, qt, preferred_element_type=F32))
        gate = jnp.where(valid, gate, -jnp.inf)
        beaten = jnp.zeros((nb, bs), F32)
        for jp in range(nb):
            row = gate[jp:jp + 1, :]
            wins = jnp.logical_or(row > gate, jnp.logical_and(row == gate, jp < blk))
            beaten = beaten + jnp.where(wins, 1.0, 0.0)
        biases.append(jnp.where(jnp.logical_and(valid, beaten < MOBA_TOPK), 0.0, MASK_VALUE))

    def scores(rows, slot):
        for h, cs in enumerate(heads):
            s_scr[slot, h] = jnp.dot(k_ref[rows, cs], qt_ref[0, cs, :],
                                     preferred_element_type=F32)

    own_slot = 1
    scores(pl.ds(start, bs), own_slot)
    scores(slice(0, bs), 0)

    for h, cs in enumerate(heads):
        s = jnp.where(kidx <= qidx, s_scr[own_slot, h], MASK_VALUE)
        m = jnp.max(s, axis=0, keepdims=True)
        p = jnp.exp2(s - m)
        m_scr[h] = m
        l_scr[h] = jnp.sum(p, axis=0, keepdims=True)
        acc_scr[h] = jnp.dot(vt_ref[qb, cs, :], p.astype(BF16), preferred_element_type=F32)

    for j in range(nb - 1):
        @pl.when(j < qb)
        def _(j=j):
            if j + 1 < nb - 1:
                scores(slice((j + 1) * bs, (j + 2) * bs), (j + 1) % 2)
            for h, cs in enumerate(heads):
                s = s_scr[j % 2, h] + biases[h][j:j + 1, :]
                m_old = m_scr[h]
                m_new = jnp.maximum(m_old, jnp.max(s, axis=0, keepdims=True))
                a = jnp.exp2(m_old - m_new)
                p = jnp.exp2(s - m_new)
                l_scr[h] = a * l_scr[h] + jnp.sum(p, axis=0, keepdims=True)
                acc_scr[h] = a * acc_scr[h] + jnp.dot(
                    vt_ref[j, cs, :], p.astype(BF16), preferred_element_type=F32)
                m_scr[h] = m_new

    for h, cs in enumerate(heads):
        o_ref[:, cs] = (acc_scr[h] * (1.0 / l_scr[h])).T.astype(BF16)


def _moba_call(qt, k, vt, kmeans, B, S, hps):
    T, HD = k.shape
    H = HD // HEAD_DIM
    bs = MOBA_BLOCK
    nb = S // bs
    hw = hps * HEAD_DIM
    return pl.pallas_call(
        functools.partial(_moba_kernel, hps=hps),
        grid=(B, H // hps, nb),
        in_specs=[
            pl.BlockSpec((1, hw, bs), lambda b, g, i: (b * nb + i, g, 0)),
            pl.BlockSpec((S, hw), lambda b, g, i: (b, g)),
            pl.BlockSpec((nb, hw, bs), lambda b, g, i: (b, g, 0)),
            pl.BlockSpec((1, nb, hw), lambda b, g, i: (b, 0, g)),
        ],
        out_specs=pl.BlockSpec((bs, hw), lambda b, g, i: (b * nb + i, g)),
        out_shape=jax.ShapeDtypeStruct((T, HD), BF16),
        scratch_shapes=[pltpu.VMEM((hps, 1, bs), F32), pltpu.VMEM((hps, 1, bs), F32),
                        pltpu.VMEM((hps, HEAD_DIM, bs), F32),
                        pltpu.VMEM((2, hps, bs, bs), F32)],
        compiler_params=_params(48, ("arbitrary", "arbitrary", "arbitrary")),
        name="moba_attention",
    )(qt, k, vt, kmeans.reshape(B, nb, HD))


def kernel(x, c, ada_w, ada_b, ln_g, ln_b, gm_w_in, gm_b_in, gm_lnv_g, gm_lnv_b, gm_w_s,
           gm_b_s, gm_w_out, kv_ada_w, kv_ada_b, w_kv, attn_w_q, attn_w_out, moe_w_router,
           moe_b_router, moe_w_gu, moe_b_gu, moe_w_down, moe_b_down):
    B, S, D = x.shape
    T = B * S
    depth = ada_w.shape[0]
    n_a = gm_w_in.shape[0]
    E = moe_w_router.shape[2]
    W = gm_w_in.shape[2] // 2
    C = gm_w_s.shape[2]
    H = attn_w_q.shape[2] // HEAD_DIM
    hps = 4 if H % 4 == 0 else 1
    assert B <= MODS_PAD_ROWS and S % MOBA_BLOCK == 0 and S % ROW_TILE == 0
    assert ROW_TILE % C == 0 and W // gm_w_s.shape[1] == LANES
    assert (D // 2) % LANES == 0 and (TOP_K_EXPERTS * T) % MOE_TILE == 0
    alpha = (2.0 * depth) ** 0.25

    c_pad = jnp.zeros((MODS_PAD_ROWS, D), F32).at[:B].set(c)
    n_mod = ada_w.shape[2] // D
    mods = _mods_call(c_pad, ada_w, ada_b).reshape(depth * MODS_PAD_ROWS * n_mod, 1, D)
    kv_mods = _mods_call(c_pad, kv_ada_w[None], kv_ada_b[None]).reshape(
        MODS_PAD_ROWS * 2, 1, D)

    def mod_row(l, which):
        return lambda b: (l * MODS_PAD_ROWS + b) * n_mod + which

    ln_g3 = ln_g.reshape(depth * 2, 1, D)
    ln_b3 = ln_b.reshape(depth * 2, 1, D)
    causal = jnp.tril(jnp.ones((C, C), F32))
    gm_w_in_b = gm_w_in.astype(BF16)
    gm_w_out_b = gm_w_out.astype(BF16)
    gm_ws_b = (gm_w_s * causal).astype(BF16)
    gm_bs_t = jnp.swapaxes(gm_b_s, 1, 2)
    attn_w_qt_b = jnp.swapaxes(attn_w_q, 1, 2).astype(BF16)
    attn_w_out_b = attn_w_out.astype(BF16)
    w_r_b = moe_w_router.astype(BF16)
    w_gu_b = moe_w_gu.astype(BF16)
    w_down_b = moe_w_down.astype(BF16)

    xt = x.reshape(T, D)
    k = vt = kmeans = None
    for l in range(depth):
        if l < n_a:
            y = _gmlp_in_call(
                xt, mods, mod_row(l, 0), mod_row(l, 1), gm_w_in_b, gm_b_in[:, None],
                gm_lnv_g[:, None], gm_lnv_b[:, None], gm_ws_b, gm_bs_t, l, S)
            w_o, lw = gm_w_out_b, l
        else:
            lw = l - n_a
            qt = _q_proj_call(xt, mods, mod_row(l, 0), mod_row(l, 1), attn_w_qt_b, lw, S,
                              HEAD_DIM ** -0.5 * LOG2_E)
            y = _moba_call(qt, k, vt, kmeans, B, S, hps)
            w_o = attn_w_out_b
        x1, xp, meta, counts = _proj_ln_router_call(
            y, w_o, lw, xt, mods, mod_row(l, 2), mod_row(l, 3), mod_row(l, 4),
            (ln_g3, ln_b3), 2 * l, w_r_b, moe_b_router[:, None], l, S, alpha)
        tile_expert, n_valid, inv = _moe_routing(meta, counts, T, E)
        ypairs = _moe_call(tile_expert, n_valid, inv, xp, w_gu_b, moe_b_gu, w_down_b,
                           moe_b_down, l, T)
        xt = _combine_ln_call(ypairs, meta, x1, mods, mod_row(l, 5), (ln_g3, ln_b3),
                              2 * l + 1, S, alpha)
        if l == n_a - 1:
            hd = w_kv.shape[1] // 2
            k, vt, kmeans = _kv_proj_call(
                xt, kv_mods, lambda b: b * 2, lambda b: b * 2 + 1,
                w_kv[:, :hd].astype(BF16), w_kv[:, hd:].T.astype(BF16), S)
    return xt.reshape(B, S, D)
```

```python
import functools

import jax
import jax.numpy as jnp
from jax import lax
from jax.experimental import pallas as pl
from jax.experimental.pallas import tpu as pltpu

HEAD_DIM = 128
MOBA_BLOCK = 256
MOBA_TOPK = 3
TOP_K_EXPERTS = 4
SWIGLU_LIMIT = 7.0
SWIGLU_ALPHA = 1.702
LN_EPS = 1e-5

LANES = 128
SUBLANES = 8

ROW_TILE = 256
MOE_TILE = 256
ACC_CHUNK = 16
MODS_PAD_ROWS = SUBLANES
MASK_VALUE = -1e30
LOG2_E = 1.4426950408889634

F32 = jnp.float32
BF16 = jnp.bfloat16
I32 = jnp.int32


def _params(vmem_mb, semantics):
    return pltpu.CompilerParams(dimension_semantics=semantics,
                                vmem_limit_bytes=vmem_mb * 1024 * 1024)


def _resident(shape):
    nd = len(shape)
    return pl.BlockSpec(shape, lambda *_: (0,) * nd, pipeline_mode=pl.Buffered(1))


def _layer_resident(tail, l):
    nd = len(tail)
    return pl.BlockSpec((1,) + tuple(tail), lambda *_: (l,) + (0,) * nd,
                        pipeline_mode=pl.Buffered(1))


def _layer_norm(r, g, b):
    mu = jnp.mean(r, axis=-1, keepdims=True)
    d = r - mu
    var = jnp.mean(d * d, axis=-1, keepdims=True)
    return d * lax.rsqrt(var + LN_EPS) * g + b


def _mods_kernel(c_ref, w_ref, b_ref, o_ref):
    c = c_ref[...]
    ca = (c * jax.nn.sigmoid(c)).astype(BF16)
    w = w_ref[0].astype(BF16)
    o_ref[0] = jnp.dot(ca, w, preferred_element_type=F32) + b_ref[0]


def _mods_call(c_pad, w, b, tn=1024):
    L, D, N = w.shape
    return pl.pallas_call(
        _mods_kernel,
        grid=(L, N // tn),
        in_specs=[
            pl.BlockSpec((MODS_PAD_ROWS, D), lambda l, j: (0, 0)),
            pl.BlockSpec((1, D, tn), lambda l, j: (l, 0, j)),
            pl.BlockSpec((1, 1, tn), lambda l, j: (l, 0, j)),
        ],
        out_specs=pl.BlockSpec((1, MODS_PAD_ROWS, tn), lambda l, j: (l, 0, j)),
        out_shape=jax.ShapeDtypeStruct((L, MODS_PAD_ROWS, N), F32),
        compiler_params=_params(40, ("arbitrary", "arbitrary")),
        name="adaln_mods",
    )(c_pad, w, b.reshape(L, 1, N))


def _gmlp_in_kernel(x_ref, sh_ref, sc_ref, w_ref, b_ref, lg_ref, lb_ref, ws_ref, bs_ref,
                    y_ref, u_scr, v_scr, vn_scr, *, ncol):
    tm, D = x_ref.shape
    W = y_ref.shape[1]
    _, G, C, _ = ws_ref.shape
    gd = W // G
    h = (x_ref[...] * (1.0 + sc_ref[0]) + sh_ref[0]).astype(BF16)
    for j in range(2 * W // ncol):
        z = jnp.dot(h, w_ref[0, :, j * ncol:(j + 1) * ncol], preferred_element_type=F32)
        z = jax.nn.gelu(z + b_ref[0, :, j * ncol:(j + 1) * ncol], approximate=True)
        if j * ncol < W:
            u_scr[:, j * ncol:(j + 1) * ncol] = z
        else:
            v_scr[:, j * ncol - W:(j + 1) * ncol - W] = z
    vn_scr[...] = _layer_norm(v_scr[...], lg_ref[0], lb_ref[0]).astype(BF16)
    for n in range(tm // C):
        rows = slice(n * C, (n + 1) * C)
        for g in range(G):
            cols = slice(g * gd, (g + 1) * gd)
            sv = jnp.dot(ws_ref[0, g], vn_scr[rows, cols], preferred_element_type=F32)
            sv = sv + bs_ref[0, :, g:g + 1]
            y_ref[rows, cols] = (u_scr[rows, cols] * sv).astype(BF16)


def _gmlp_in_call(x, mods, sh_row, sc_row, w_in, b_in, lnv_g, lnv_b, ws, bs_t, l, S):
    T, D = x.shape
    W = w_in.shape[2] // 2
    _, G, C, _ = ws.shape
    tm = ROW_TILE
    tpb = S // tm
    mod = lambda row: pl.BlockSpec((1, 1, D), lambda i: (row(i // tpb), 0, 0))
    return pl.pallas_call(
        functools.partial(_gmlp_in_kernel, ncol=512),
        grid=(T // tm,),
        in_specs=[
            pl.BlockSpec((tm, D), lambda i: (i, 0)),
            mod(sh_row), mod(sc_row),
            _layer_resident((D, 2 * W), l),
            _layer_resident((1, 2 * W), l),
            _layer_resident((1, W), l), _layer_resident((1, W), l),
            _layer_resident((G, C, C), l),
            _layer_resident((C, G), l),
        ],
        out_specs=pl.BlockSpec((tm, W), lambda i: (i, 0)),
        out_shape=jax.ShapeDtypeStruct((T, W), BF16),
        scratch_shapes=[pltpu.VMEM((tm, W), F32), pltpu.VMEM((tm, W), F32),
                        pltpu.VMEM((tm, W), BF16)],
        compiler_params=_params(48, ("arbitrary",)),
        name="gmlp_in",
    )(x, mods, mods, w_in, b_in, lnv_g, lnv_b, ws, bs_t)


def _proj_ln_router_kernel(a_ref, w_ref, x_ref, g1_ref, lg_ref, lb_ref, sh_ref, sc_ref,
                           wr_ref, br_ref,
                           x1_ref, xp_ref, meta_ref, cnt_ref, carry, f_scr, *, alpha):
    tm, D = x_ref.shape
    E = wr_ref.shape[2]
    rpt = xp_ref.shape[0] // tm
    i = pl.program_id(0)

    @pl.when(i == 0)
    def _():
        carry[...] = jnp.zeros_like(carry)

    h = jnp.dot(a_ref[...], w_ref[0], preferred_element_type=F32)
    x1 = _layer_norm(alpha * x_ref[...] + (1.0 + g1_ref[0]) * h, lg_ref[0], lb_ref[0])
    x1_ref[...] = x1
    xm = x1 * (1.0 + sc_ref[0]) + sh_ref[0]
    for c in range(rpt):
        f_scr[pl.ds(c, tm, stride=rpt), :] = xm[:, c * LANES:(c + 1) * LANES]
    xp_ref[...] = f_scr[...].astype(BF16)

    logits = jnp.dot(xm.astype(BF16), wr_ref[0], preferred_element_type=F32) + br_ref[0]
    lane = lax.broadcasted_iota(I32, (tm, E), 1).astype(F32)
    onehots, vals, idxs = [], [], []
    cur = logits
    for _k in range(TOP_K_EXPERTS):
        m = jnp.max(cur, axis=-1, keepdims=True)
        idx = jnp.min(jnp.where(cur == m, lane, float(E)), axis=-1, keepdims=True)
        oh = lane == idx
        cur = jnp.where(oh, -jnp.inf, cur)
        onehots.append(jnp.where(oh, 1.0, 0.0))
        vals.append(m)
        idxs.append(idx.astype(I32))
    exps = [jnp.exp(v - vals[0]) for v in vals]
    denom = exps[0]
    for e in exps[1:]:
        denom = denom + e
    oh_all = onehots[0]
    for oh in onehots[1:]:
        oh_all = oh_all + oh
    r_i = lax.broadcasted_iota(I32, (tm, tm), 0)
    c_i = lax.broadcasted_iota(I32, (tm, tm), 1)
    tri = jnp.where(r_i > c_i, 1.0, 0.0).astype(BF16)
    before = jnp.dot(tri, oh_all.astype(BF16), preferred_element_type=F32) + carry[...]
    mlane = lax.broadcasted_iota(I32, (tm, LANES), 1)
    meta = jnp.zeros((tm, LANES), I32)
    for k in range(TOP_K_EXPERTS):
        rank = jnp.sum(onehots[k] * before, axis=-1, keepdims=True).astype(I32)
        wk = lax.bitcast_convert_type(exps[k] / denom, I32)
        meta = jnp.where(mlane == k, idxs[k], meta)
        meta = jnp.where(mlane == TOP_K_EXPERTS + k, wk, meta)
        meta = jnp.where(mlane == 2 * TOP_K_EXPERTS + k, rank, meta)
    meta_ref[...] = meta
    carry[...] = carry[...] + jnp.sum(oh_all, axis=0, keepdims=True)
    cnt_ref[...] = carry[...].astype(I32)


def _proj_ln_router_call(a, w, lw, x, mods, g1_row, sh_row, sc_row, ln, ln_row, w_r, b_r, l,
                         S, alpha):
    T, D = x.shape
    K = a.shape[1]
    E = w_r.shape[2]
    rpt = D // LANES
    tm = ROW_TILE
    tpb = S // tm
    mod = lambda row: pl.BlockSpec((1, 1, D), lambda i: (row(i // tpb), 0, 0))
    lnspec = pl.BlockSpec((1, 1, D), lambda i: (ln_row, 0, 0))
    return pl.pallas_call(
        functools.partial(_proj_ln_router_kernel, alpha=alpha),
        grid=(T // tm,),
        in_specs=[
            pl.BlockSpec((tm, K), lambda i: (i, 0)),
            _layer_resident((K, D), lw),
            pl.BlockSpec((tm, D), lambda i: (i, 0)),
            mod(g1_row), lnspec, lnspec,
            mod(sh_row), mod(sc_row),
            _layer_resident((D, E), l), _layer_resident((1, E), l),
        ],
        out_specs=[
            pl.BlockSpec((tm, D), lambda i: (i, 0)),
            pl.BlockSpec((tm * rpt, LANES), lambda i: (i, 0)),
            pl.BlockSpec((tm, LANES), lambda i: (i, 0)),
            pl.BlockSpec((1, E), lambda i: (0, 0)),
        ],
        out_shape=[
            jax.ShapeDtypeStruct((T, D), F32),
            jax.ShapeDtypeStruct((T * rpt, LANES), BF16),
            jax.ShapeDtypeStruct((T, LANES), I32),
            jax.ShapeDtypeStruct((1, E), I32),
        ],
        scratch_shapes=[pltpu.VMEM((1, E), F32), pltpu.VMEM((tm * rpt, LANES), F32)],
        compiler_params=_params(48, ("arbitrary",)),
        name="proj_ln_router",
    )(a, w, x, mods, ln[0], ln[1], mods, mods, w_r, b_r)


def _moe_up_kernel(te_ref, nv_ref, src_ref, xp_ref, wgu_ref, bgu_ref, act_ref,
                   g_scr, f_scr, *, rpt):
    tm, F = act_ref.shape
    i = pl.program_id(0)
    nt = pl.num_programs(0)
    nv = nv_ref[0]

    def gather(tile, slot):
        base = tile * tm
        for r in range(tm):
            src = pl.multiple_of(src_ref[base + r], rpt)
            g_scr[slot, r * rpt:(r + 1) * rpt, :] = xp_ref[pl.ds(src, rpt), :]

    @pl.when(i == 0)
    def _():
        gather(0, 0)

    @pl.when(i < nv)
    def _():
        slot = i % 2
        f_scr[...] = g_scr[slot].astype(F32)
        xt = jnp.concatenate(
            [f_scr[pl.ds(c, tm, stride=rpt), :] for c in range(rpt)], axis=1).astype(BF16)
        gather(jnp.minimum(i + 1, nt - 1), 1 - slot)
        h = jnp.dot(xt, wgu_ref[0, 0], preferred_element_type=F32) + bgu_ref[0, 0]
        g = jnp.minimum(h[:, :F], SWIGLU_LIMIT)
        u = jnp.clip(h[:, F:], -SWIGLU_LIMIT, SWIGLU_LIMIT)
        act_ref[...] = ((u + 1.0) * g * jax.nn.sigmoid(SWIGLU_ALPHA * g)).astype(BF16)

    @pl.when(i >= nv)
    def _():
        act_ref[...] = jnp.zeros_like(act_ref)


def _moe_up_call(tile_expert, n_valid, src_row, xp, w_gu, b_gu, l):
    L, E, D, F2 = w_gu.shape
    rpt = D // LANES
    tm = MOE_TILE
    nt = tile_expert.shape[0]
    grid_spec = pltpu.PrefetchScalarGridSpec(
        num_scalar_prefetch=3,
        grid=(nt,),
        in_specs=[
            pl.BlockSpec(xp.shape, lambda i, te, nv, src: (0, 0), pipeline_mode=pl.Buffered(1)),
            pl.BlockSpec((1, 1, D, F2), lambda i, te, nv, src: (l, te[i], 0, 0)),
            pl.BlockSpec((1, 1, 1, F2), lambda i, te, nv, src: (l, te[i], 0, 0)),
        ],
        out_specs=pl.BlockSpec((tm, F2 // 2), lambda i, te, nv, src: (i, 0)),
        scratch_shapes=[
            pltpu.VMEM((2, tm * rpt, LANES), BF16),
            pltpu.VMEM((tm * rpt, LANES), F32),
        ],
    )
    return pl.pallas_call(
        functools.partial(_moe_up_kernel, rpt=rpt),
        grid_spec=grid_spec,
        out_shape=jax.ShapeDtypeStruct((nt * tm, F2 // 2), BF16),
        compiler_params=_params(58, ("arbitrary",)),
        name="moe_up",
    )(tile_expert, n_valid, src_row, xp, w_gu, b_gu.reshape(L, E, 1, F2))


def _moe_down_kernel(te_ref, nv_ref, dst_ref, gate_ref, act_ref, wd_ref, bd_ref, o_hbm,
                     o_scr, y_scr, st_scr, sem):
    tm = act_ref.shape[0]
    spt = y_scr.shape[1] // tm
    i = pl.program_id(0)
    last = pl.num_programs(0) - 1
    nv = nv_ref[0]

    @pl.when(i == 0)
    def _():
        o_scr[...] = jnp.zeros_like(o_scr)

    def matmul(slot):
        y = jnp.dot(act_ref[...], wd_ref[0, 0].astype(BF16), preferred_element_type=F32)
        y = y + bd_ref[0, 0]
        for s in range(spt):
            y_scr[slot, pl.ds(s, tm, stride=spt), :] = y[:, s * LANES:(s + 1) * LANES]

    def accumulate(tile, slot):
        base = tile * tm
        for c0 in range(0, tm, ACC_CHUNK):
            dsts = [pl.multiple_of(dst_ref[base + r], spt) for r in range(c0, c0 + ACC_CHUNK)]
            for k, r in enumerate(range(c0, c0 + ACC_CHUNK)):
                rows = slice(r * spt, (r + 1) * spt)
                st_scr[rows, :] = (o_scr[pl.ds(dsts[k], spt), :]
                                   + gate_ref[base + r] * y_scr[slot, rows, :])
            for k, r in enumerate(range(c0, c0 + ACC_CHUNK)):
                o_scr[pl.ds(dsts[k], spt), :] = st_scr[r * spt:(r + 1) * spt, :]

    @pl.when(jnp.logical_and(i == 0, i < nv))
    def _():
        matmul(0)

    @pl.when(jnp.logical_and(i >= 1, i < nv))
    def _():
        accumulate(i - 1, (i - 1) % 2)
        matmul(i % 2)

    @pl.when(jnp.logical_and(i >= 1, i == nv))
    def _():
        accumulate(i - 1, (i - 1) % 2)

    @pl.when(i == last)
    def _():
        out = pltpu.make_async_copy(o_scr, o_hbm, sem)
        out.start()
        out.wait()


def _moe_down_call(tile_expert, n_valid, dst_row, gate_sorted, act, w_down, b_down, l, half,
                   n_tokens):
    L, E, F, D = w_down.shape
    dh = D // 2
    spt = dh // LANES
    tm = MOE_TILE
    nt = tile_expert.shape[0]
    n_rows = (n_tokens + 1) * spt
    tile = lambda i: jnp.minimum(i, nt - 1)
    grid_spec = pltpu.PrefetchScalarGridSpec(
        num_scalar_prefetch=4,
        grid=(nt + 1,),
        in_specs=[
            pl.BlockSpec((tm, F), lambda i, te, nv, dst, gt: (tile(i), 0)),
            pl.BlockSpec((1, 1, F, dh), lambda i, te, nv, dst, gt: (l, te[tile(i)], 0, half)),
            pl.BlockSpec((1, 1, 1, dh), lambda i, te, nv, dst, gt: (l, te[tile(i)], 0, half)),
        ],
        out_specs=pl.BlockSpec(memory_space=pl.ANY),
        scratch_shapes=[
            pltpu.VMEM((n_rows, LANES), F32),
            pltpu.VMEM((2, tm * spt, LANES), F32),
            pltpu.VMEM((tm * spt, LANES), F32),
            pltpu.SemaphoreType.DMA(()),
        ],
    )
    return pl.pallas_call(
        _moe_down_kernel,
        grid_spec=grid_spec,
        out_shape=jax.ShapeDtypeStruct((n_rows, LANES), F32),
        compiler_params=_params(56, ("arbitrary",)),
        name="moe_down",
    )(tile_expert, n_valid, dst_row, gate_sorted, act, w_down, b_down.reshape(L, E, 1, D))


def _moe_routing(meta, counts, n_tokens, n_experts, d_model):
    tm = MOE_TILE
    n_pairs = TOP_K_EXPERTS * n_tokens
    nt = n_pairs // tm + n_experts
    ei = meta[:, 0:TOP_K_EXPERTS]
    tw = lax.bitcast_convert_type(meta[:, TOP_K_EXPERTS:2 * TOP_K_EXPERTS], F32)
    rk = meta[:, 2 * TOP_K_EXPERTS:3 * TOP_K_EXPERTS]
    cnt = counts[0]
    tiles = (cnt + tm - 1) // tm
    tile_end = jnp.cumsum(tiles)
    group_start = (tile_end - tiles) * tm
    dest = (group_start[ei] + rk).reshape(-1)
    inv = jnp.full((nt * tm,), -1, I32).at[dest].set(jnp.arange(n_pairs, dtype=I32))
    live = inv >= 0
    pair = jnp.maximum(inv, 0)
    tok = pair // TOP_K_EXPERTS
    src_row = jnp.where(live, tok * (d_model // LANES), 0)
    dst_row = jnp.where(live, tok, n_tokens) * (d_model // 2 // LANES)
    gate_sorted = jnp.where(live, tw.reshape(-1)[pair], 0.0)
    n_valid = tile_end[-1:].astype(I32)
    tid = jnp.minimum(jnp.arange(nt, dtype=I32), n_valid[0] - 1)
    tile_expert = jnp.sum((tile_end[None, :] <= tid[:, None]).astype(I32), axis=1)
    return tile_expert, n_valid, src_row, dst_row, gate_sorted


def _moe_res_ln_kernel(oa_ref, ob_ref, x1_ref, g2_ref, lg_ref, lb_ref, o_ref, *, alpha):
    tc = x1_ref.shape[0]
    spt = oa_ref.shape[0] // tc
    parts = [ref[pl.ds(s, tc, stride=spt), :] for ref in (oa_ref, ob_ref) for s in range(spt)]
    moe = jnp.concatenate(parts, axis=1)
    r = alpha * x1_ref[...] + (1.0 + g2_ref[0]) * moe
    o_ref[...] = _layer_norm(r, lg_ref[0], lb_ref[0])


def _moe_res_ln_call(o_a, o_b, x1, mods, g2_row, ln, ln_row, S, alpha):
    T, D = x1.shape
    spt = (D // 2) // LANES
    tc = ROW_TILE
    tpb = S // tc
    half = pl.BlockSpec((tc * spt, LANES), lambda i: (i, 0))
    lnspec = pl.BlockSpec((1, 1, D), lambda i: (ln_row, 0, 0))
    return pl.pallas_call(
        functools.partial(_moe_res_ln_kernel, alpha=alpha),
        grid=(T // tc,),
        in_specs=[half, half,
                  pl.BlockSpec((tc, D), lambda i: (i, 0)),
                  pl.BlockSpec((1, 1, D), lambda i: (g2_row(i // tpb), 0, 0)),
                  lnspec, lnspec],
        out_specs=pl.BlockSpec((tc, D), lambda i: (i, 0)),
        out_shape=jax.ShapeDtypeStruct((T, D), F32),
        compiler_params=_params(40, ("arbitrary",)),
        name="moe_res_ln",
    )(o_a, o_b, x1, mods, ln[0], ln[1])


def _q_proj_kernel(x_ref, sh_ref, sc_ref, wt_ref, o_ref, *, out_scale):
    h = (x_ref[...] * (1.0 + sc_ref[0]) + sh_ref[0]).astype(BF16)
    zt = lax.dot_general(wt_ref[0], h, (((1,), (1,)), ((), ())), preferred_element_type=F32)
    o_ref[0] = (zt * out_scale).astype(BF16)


def _q_proj_call(x, mods, sh_row, sc_row, w_t, l, S, out_scale):
    T, D = x.shape
    N = w_t.shape[1]
    tm = MOBA_BLOCK
    tpb = S // tm
    mod = lambda row: pl.BlockSpec((1, 1, D), lambda i: (row(i // tpb), 0, 0))
    return pl.pallas_call(
        functools.partial(_q_proj_kernel, out_scale=out_scale),
        grid=(T // tm,),
        in_specs=[pl.BlockSpec((tm, D), lambda i: (i, 0)), mod(sh_row), mod(sc_row),
                  _layer_resident((N, D), l)],
        out_specs=pl.BlockSpec((1, N, tm), lambda i: (i, 0, 0)),
        out_shape=jax.ShapeDtypeStruct((T // tm, N, tm), BF16),
        compiler_params=_params(48, ("arbitrary",)),
        name="q_proj",
    )(x, mods, mods, w_t)


def _kv_proj_kernel(x_ref, sh_ref, sc_ref, wk_ref, wvt_ref, k_ref, vt_ref, km_ref):
    h = (x_ref[...] * (1.0 + sc_ref[0]) + sh_ref[0]).astype(BF16)
    zk = jnp.dot(h, wk_ref[...], preferred_element_type=F32)
    k_ref[...] = zk.astype(BF16)
    km_ref[0] = jnp.mean(zk, axis=0, keepdims=True)
    zvt = lax.dot_general(wvt_ref[...], h, (((1,), (1,)), ((), ())),
                          preferred_element_type=F32)
    vt_ref[0] = zvt.astype(BF16)


def _kv_proj_call(x, mods, sh_row, sc_row, w_k, w_vt, S):
    T, D = x.shape
    HD = w_k.shape[1]
    tm = MOBA_BLOCK
    tpb = S // tm
    mod = lambda row: pl.BlockSpec((1, 1, D), lambda i: (row(i // tpb), 0, 0))
    return pl.pallas_call(
        _kv_proj_kernel,
        grid=(T // tm,),
        in_specs=[pl.BlockSpec((tm, D), lambda i: (i, 0)), mod(sh_row), mod(sc_row),
                  _resident((D, HD)), _resident((HD, D))],
        out_specs=[pl.BlockSpec((tm, HD), lambda i: (i, 0)),
                   pl.BlockSpec((1, HD, tm), lambda i: (i, 0, 0)),
                   pl.BlockSpec((1, 1, HD), lambda i: (i, 0, 0))],
        out_shape=[jax.ShapeDtypeStruct((T, HD), BF16),
                   jax.ShapeDtypeStruct((T // tm, HD, tm), BF16),
                   jax.ShapeDtypeStruct((T // tm, 1, HD), F32)],
        compiler_params=_params(48, ("arbitrary",)),
        name="kv_proj",
    )(x, mods, mods, w_k, w_vt)


def _moba_kernel(qt_ref, k_ref, vt_ref, km_ref, o_ref, m_scr, l_scr, acc_scr, s_scr, *, hps):
    bs = qt_ref.shape[2]
    nb = km_ref.shape[1]
    qb = pl.program_id(2)
    start = pl.multiple_of(qb * bs, bs)
    kidx = lax.broadcasted_iota(I32, (bs, bs), 0)
    qidx = lax.broadcasted_iota(I32, (bs, bs), 1)
    blk = lax.broadcasted_iota(I32, (nb, bs), 0)
    valid = blk < qb
    heads = [slice(h * HEAD_DIM, (h + 1) * HEAD_DIM) for h in range(hps)]

    biases = []
    for cs in heads:
        qt = qt_ref[0, cs, :]
        km = km_ref[0, :, cs]
        km_hi = km.astype(BF16)
        km_lo = (km - km_hi.astype(F32)).astype(BF16)
        gate = (jnp.dot(km_hi, qt, preferred_element_type=F32)
                + jnp.dot(km_lo, qt, preferred_element_type=F32))
        gate = jnp.where(valid, gate, -jnp.inf)
        beaten = jnp.zeros((nb, bs), F32)
        for jp in range(nb):
            row = gate[jp:jp + 1, :]
            wins = jnp.logical_or(row > gate, jnp.logical_and(row == gate, jp < blk))
            beaten = beaten + jnp.where(wins, 1.0, 0.0)
        biases.append(jnp.where(jnp.logical_and(valid, beaten < MOBA_TOPK), 0.0, MASK_VALUE))

    def scores(rows, slot):
        for h, cs in enumerate(heads):
            s_scr[slot, h] = jnp.dot(k_ref[rows, cs], qt_ref[0, cs, :],
                                     preferred_element_type=F32)

    own_slot = 1
    scores(pl.ds(start, bs), own_slot)
    scores(slice(0, bs), 0)

    for h, cs in enumerate(heads):
        s = jnp.where(kidx <= qidx, s_scr[own_slot, h], MASK_VALUE)
        m = jnp.max(s, axis=0, keepdims=True)
        p = jnp.exp2(s - m)
        m_scr[h] = m
        l_scr[h] = jnp.sum(p, axis=0, keepdims=True)
        acc_scr[h] = jnp.dot(vt_ref[qb, cs, :], p.astype(BF16), preferred_element_type=F32)

    for j in range(nb - 1):
        @pl.when(j < qb)
        def _(j=j):
            if j + 1 < nb - 1:
                scores(slice((j + 1) * bs, (j + 2) * bs), (j + 1) % 2)
            for h, cs in enumerate(heads):
                s = s_scr[j % 2, h] + biases[h][j:j + 1, :]
                m_old = m_scr[h]
                m_new = jnp.maximum(m_old, jnp.max(s, axis=0, keepdims=True))
                a = jnp.exp2(m_old - m_new)
                p = jnp.exp2(s - m_new)
                l_scr[h] = a * l_scr[h] + jnp.sum(p, axis=0, keepdims=True)
                acc_scr[h] = a * acc_scr[h] + jnp.dot(
                    vt_ref[j, cs, :], p.astype(BF16), preferred_element_type=F32)
                m_scr[h] = m_new

    for h, cs in enumerate(heads):
        o_ref[:, cs] = (acc_scr[h] * (1.0 / l_scr[h])).T.astype(BF16)


def _moba_call(qt, k, vt, kmeans, B, S, hps):
    T, HD = k.shape
    H = HD // HEAD_DIM
    bs = MOBA_BLOCK
    nb = S // bs
    hw = hps * HEAD_DIM
    return pl.pallas_call(
        functools.partial(_moba_kernel, hps=hps),
        grid=(B, H // hps, nb),
        in_specs=[
            pl.BlockSpec((1, hw, bs), lambda b, g, i: (b * nb + i, g, 0)),
            pl.BlockSpec((S, hw), lambda b, g, i: (b, g)),
            pl.BlockSpec((nb, hw, bs), lambda b, g, i: (b, g, 0)),
            pl.BlockSpec((1, nb, hw), lambda b, g, i: (b, 0, g)),
        ],
        out_specs=pl.BlockSpec((bs, hw), lambda b, g, i: (b * nb + i, g)),
        out_shape=jax.ShapeDtypeStruct((T, HD), BF16),
        scratch_shapes=[pltpu.VMEM((hps, 1, bs), F32), pltpu.VMEM((hps, 1, bs), F32),
                        pltpu.VMEM((hps, HEAD_DIM, bs), F32),
                        pltpu.VMEM((2, hps, bs, bs), F32)],
        compiler_params=_params(48, ("arbitrary", "arbitrary", "arbitrary")),
        name="moba_attention",
    )(qt, k, vt, kmeans.reshape(B, nb, HD))


def kernel(x, c, ada_w, ada_b, ln_g, ln_b, gm_w_in, gm_b_in, gm_lnv_g, gm_lnv_b, gm_w_s,
           gm_b_s, gm_w_out, kv_ada_w, kv_ada_b, w_kv, attn_w_q, attn_w_out, moe_w_router,
           moe_b_router, moe_w_gu, moe_b_gu, moe_w_down, moe_b_down):
    B, S, D = x.shape
    T = B * S
    depth = ada_w.shape[0]
    n_a = gm_w_in.shape[0]
    E = moe_w_router.shape[2]
    W = gm_w_in.shape[2] // 2
    C = gm_w_s.shape[2]
    H = attn_w_q.shape[2] // HEAD_DIM
    hps = 4 if H % 4 == 0 else 1
    assert B <= MODS_PAD_ROWS and S % MOBA_BLOCK == 0 and S % ROW_TILE == 0
    assert ROW_TILE % C == 0 and W // gm_w_s.shape[1] == LANES
    assert (D // 2) % LANES == 0 and (TOP_K_EXPERTS * T) % MOE_TILE == 0
    alpha = (2.0 * depth) ** 0.25

    c_pad = jnp.zeros((MODS_PAD_ROWS, D), F32).at[:B].set(c)
    n_mod = ada_w.shape[2] // D
    mods = _mods_call(c_pad, ada_w, ada_b).reshape(depth * MODS_PAD_ROWS * n_mod, 1, D)
    kv_mods = _mods_call(c_pad, kv_ada_w[None], kv_ada_b[None]).reshape(
        MODS_PAD_ROWS * 2, 1, D)

    def mod_row(l, which):
        return lambda b: (l * MODS_PAD_ROWS + b) * n_mod + which

    ln_g3 = ln_g.reshape(depth * 2, 1, D)
    ln_b3 = ln_b.reshape(depth * 2, 1, D)
    causal = jnp.tril(jnp.ones((C, C), F32))
    gm_w_in_b = gm_w_in.astype(BF16)
    gm_w_out_b = gm_w_out.astype(BF16)
    gm_ws_b = (gm_w_s * causal).astype(BF16)
    gm_bs_t = jnp.swapaxes(gm_b_s, 1, 2)
    attn_w_qt_b = jnp.swapaxes(attn_w_q, 1, 2).astype(BF16)
    attn_w_out_b = attn_w_out.astype(BF16)
    w_r_b = moe_w_router.astype(BF16)
    w_gu_b = moe_w_gu.astype(BF16)

    xt = x.reshape(T, D)
    k = vt = kmeans = None
    for l in range(depth):
        if l < n_a:
            y = _gmlp_in_call(
                xt, mods, mod_row(l, 0), mod_row(l, 1), gm_w_in_b, gm_b_in[:, None],
                gm_lnv_g[:, None], gm_lnv_b[:, None], gm_ws_b, gm_bs_t, l, S)
            w_o, lw = gm_w_out_b, l
        else:
            lw = l - n_a
            qt = _q_proj_call(xt, mods, mod_row(l, 0), mod_row(l, 1), attn_w_qt_b, lw, S,
                              HEAD_DIM ** -0.5 * LOG2_E)
            y = _moba_call(qt, k, vt, kmeans, B, S, hps)
            w_o = attn_w_out_b
        x1, xp, meta, counts = _proj_ln_router_call(
            y, w_o, lw, xt, mods, mod_row(l, 2), mod_row(l, 3), mod_row(l, 4),
            (ln_g3, ln_b3), 2 * l, w_r_b, moe_b_router[:, None], l, S, alpha)
        tile_expert, n_valid, src_row, dst_row, gate_sorted = _moe_routing(
            meta, counts, T, E, D)
        act = _moe_up_call(tile_expert, n_valid, src_row, xp, w_gu_b, moe_b_gu, l)
        halves = [_moe_down_call(tile_expert, n_valid, dst_row, gate_sorted, act, moe_w_down,
                                 moe_b_down, l, half, T) for half in range(2)]
        xt = _moe_res_ln_call(halves[0], halves[1], x1, mods, mod_row(l, 5), (ln_g3, ln_b3),
                              2 * l + 1, S, alpha)
        if l == n_a - 1:
            hd = w_kv.shape[1] // 2
            k, vt, kmeans = _kv_proj_call(
                xt, kv_mods, lambda b: b * 2, lambda b: b * 2 + 1,
                w_kv[:, :hd].astype(BF16), w_kv[:, hd:].T.astype(BF16), S)
    return xt.reshape(B, S, D)
```

```python
import functools

import jax
import jax.numpy as jnp
from jax import lax
from jax.experimental import pallas as pl
from jax.experimental.pallas import tpu as pltpu

HEAD_DIM = 128
MOBA_BLOCK = 256
MOBA_TOPK = 3
TOP_K_EXPERTS = 4
SWIGLU_LIMIT = 7.0
SWIGLU_ALPHA = 1.702
LN_EPS = 1e-5

LANES = 128
SUBLANES = 8

ROW_TILE = 256
MOE_TILE = 512
ACC_CHUNK = 16
MODS_PAD_ROWS = SUBLANES
MASK_VALUE = -1e30
LOG2_E = 1.4426950408889634
HI16 = -65536

F32 = jnp.float32
BF16 = jnp.bfloat16
I32 = jnp.int32


def _params(vmem_mb, semantics):
    return pltpu.CompilerParams(dimension_semantics=semantics,
                                vmem_limit_bytes=vmem_mb * 1024 * 1024)


def _resident(shape):
    nd = len(shape)
    return pl.BlockSpec(shape, lambda *_: (0,) * nd, pipeline_mode=pl.Buffered(1))


def _layer_resident(tail, l):
    nd = len(tail)
    return pl.BlockSpec((1,) + tuple(tail), lambda *_: (l,) + (0,) * nd,
                        pipeline_mode=pl.Buffered(1))


def _layer_norm(r, g, b):
    mu = jnp.mean(r, axis=-1, keepdims=True)
    d = r - mu
    var = jnp.mean(d * d, axis=-1, keepdims=True)
    return d * lax.rsqrt(var + LN_EPS) * g + b


def _pack_bf16_pair(lo, hi):
    lo = lax.bitcast_convert_type(lo.astype(BF16).astype(F32), I32)
    hi = lax.bitcast_convert_type(hi.astype(BF16).astype(F32), I32)
    return lax.shift_right_logical(lo, 16) | (hi & HI16)


def _unpack_bf16_pair(w):
    lo = lax.bitcast_convert_type(lax.shift_left(w, 16), F32)
    hi = lax.bitcast_convert_type(w & HI16, F32)
    return lo, hi


def _mods_kernel(c_ref, w_ref, b_ref, o_ref):
    c = c_ref[...]
    ca = (c * jax.nn.sigmoid(c)).astype(BF16)
    w = w_ref[0].astype(BF16)
    o_ref[0] = jnp.dot(ca, w, preferred_element_type=F32) + b_ref[0]


def _mods_call(c_pad, w, b, tn=1024):
    L, D, N = w.shape
    return pl.pallas_call(
        _mods_kernel,
        grid=(L, N // tn),
        in_specs=[
            pl.BlockSpec((MODS_PAD_ROWS, D), lambda l, j: (0, 0)),
            pl.BlockSpec((1, D, tn), lambda l, j: (l, 0, j)),
            pl.BlockSpec((1, 1, tn), lambda l, j: (l, 0, j)),
        ],
        out_specs=pl.BlockSpec((1, MODS_PAD_ROWS, tn), lambda l, j: (l, 0, j)),
        out_shape=jax.ShapeDtypeStruct((L, MODS_PAD_ROWS, N), F32),
        compiler_params=_params(40, ("arbitrary", "arbitrary")),
        name="adaln_mods",
    )(c_pad, w, b.reshape(L, 1, N))


def _gmlp_in_kernel(x_ref, sh_ref, sc_ref, w_ref, b_ref, lg_ref, lb_ref, ws_ref, bs_ref,
                    y_ref, u_scr, v_scr, vn_scr, *, ncol):
    tm, D = x_ref.shape
    W = y_ref.shape[1]
    _, G, C, _ = ws_ref.shape
    gd = W // G
    h = (x_ref[...] * (1.0 + sc_ref[0]) + sh_ref[0]).astype(BF16)
    for j in range(2 * W // ncol):
        z = jnp.dot(h, w_ref[0, :, j * ncol:(j + 1) * ncol], preferred_element_type=F32)
        z = jax.nn.gelu(z + b_ref[0, :, j * ncol:(j + 1) * ncol], approximate=True)
        if j * ncol < W:
            u_scr[:, j * ncol:(j + 1) * ncol] = z
        else:
            v_scr[:, j * ncol - W:(j + 1) * ncol - W] = z
    vn_scr[...] = _layer_norm(v_scr[...], lg_ref[0], lb_ref[0]).astype(BF16)
    for n in range(tm // C):
        rows = slice(n * C, (n + 1) * C)
        for g in range(G):
            cols = slice(g * gd, (g + 1) * gd)
            sv = jnp.dot(ws_ref[0, g], vn_scr[rows, cols], preferred_element_type=F32)
            sv = sv + bs_ref[0, :, g:g + 1]
            y_ref[rows, cols] = (u_scr[rows, cols] * sv).astype(BF16)


def _gmlp_in_call(x, mods, sh_row, sc_row, w_in, b_in, lnv_g, lnv_b, ws, bs_t, l, S):
    T, D = x.shape
    W = w_in.shape[2] // 2
    _, G, C, _ = ws.shape
    tm = ROW_TILE
    tpb = S // tm
    mod = lambda row: pl.BlockSpec((1, 1, D), lambda i: (row(i // tpb), 0, 0))
    return pl.pallas_call(
        functools.partial(_gmlp_in_kernel, ncol=512),
        grid=(T // tm,),
        in_specs=[
            pl.BlockSpec((tm, D), lambda i: (i, 0)),
            mod(sh_row), mod(sc_row),
            _layer_resident((D, 2 * W), l),
            _layer_resident((1, 2 * W), l),
            _layer_resident((1, W), l), _layer_resident((1, W), l),
            _layer_resident((G, C, C), l),
            _layer_resident((C, G), l),
        ],
        out_specs=pl.BlockSpec((tm, W), lambda i: (i, 0)),
        out_shape=jax.ShapeDtypeStruct((T, W), BF16),
        scratch_shapes=[pltpu.VMEM((tm, W), F32), pltpu.VMEM((tm, W), F32),
                        pltpu.VMEM((tm, W), BF16)],
        compiler_params=_params(48, ("arbitrary",)),
        name="gmlp_in",
    )(x, mods, mods, w_in, b_in, lnv_g, lnv_b, ws, bs_t)


def _proj_ln_router_kernel(a_ref, w_ref, x_ref, g1_ref, lg_ref, lb_ref, sh_ref, sc_ref,
                           wr_ref, br_ref,
                           x1_ref, xp_ref, meta_ref, cnt_ref, carry, *, alpha):
    tm, D = x_ref.shape
    E = wr_ref.shape[2]
    spt = xp_ref.shape[0] // tm
    i = pl.program_id(0)

    @pl.when(i == 0)
    def _():
        carry[...] = jnp.zeros_like(carry)

    h = jnp.dot(a_ref[...], w_ref[0], preferred_element_type=F32)
    x1 = _layer_norm(alpha * x_ref[...] + (1.0 + g1_ref[0]) * h, lg_ref[0], lb_ref[0])
    x1_ref[...] = x1
    xm = x1 * (1.0 + sc_ref[0]) + sh_ref[0]
    pk = _pack_bf16_pair(xm[:, :D // 2], xm[:, D // 2:])
    for s in range(spt):
        xp_ref[pl.ds(s, tm, stride=spt), :] = pk[:, s * LANES:(s + 1) * LANES]

    logits = jnp.dot(xm.astype(BF16), wr_ref[0], preferred_element_type=F32) + br_ref[0]
    lane = lax.broadcasted_iota(I32, (tm, E), 1).astype(F32)
    onehots, vals, idxs = [], [], []
    cur = logits
    for _k in range(TOP_K_EXPERTS):
        m = jnp.max(cur, axis=-1, keepdims=True)
        idx = jnp.min(jnp.where(cur == m, lane, float(E)), axis=-1, keepdims=True)
        oh = lane == idx
        cur = jnp.where(oh, -jnp.inf, cur)
        onehots.append(jnp.where(oh, 1.0, 0.0))
        vals.append(m)
        idxs.append(idx.astype(I32))
    exps = [jnp.exp(v - vals[0]) for v in vals]
    denom = exps[0]
    for e in exps[1:]:
        denom = denom + e
    oh_all = onehots[0]
    for oh in onehots[1:]:
        oh_all = oh_all + oh
    r_i = lax.broadcasted_iota(I32, (tm, tm), 0)
    c_i = lax.broadcasted_iota(I32, (tm, tm), 1)
    tri = jnp.where(r_i > c_i, 1.0, 0.0).astype(BF16)
    before = jnp.dot(tri, oh_all.astype(BF16), preferred_element_type=F32) + carry[...]
    mlane = lax.broadcasted_iota(I32, (tm, LANES), 1)
    meta = jnp.zeros((tm, LANES), I32)
    for k in range(TOP_K_EXPERTS):
        rank = jnp.sum(onehots[k] * before, axis=-1, keepdims=True).astype(I32)
        wk = lax.bitcast_convert_type(exps[k] / denom, I32)
        meta = jnp.where(mlane == k, idxs[k], meta)
        meta = jnp.where(mlane == TOP_K_EXPERTS + k, wk, meta)
        meta = jnp.where(mlane == 2 * TOP_K_EXPERTS + k, rank, meta)
    meta_ref[...] = meta
    carry[...] = carry[...] + jnp.sum(oh_all, axis=0, keepdims=True)
    cnt_ref[...] = carry[...].astype(I32)


def _proj_ln_router_call(a, w, lw, x, mods, g1_row, sh_row, sc_row, ln, ln_row, w_r, b_r, l,
                         S, alpha):
    T, D = x.shape
    K = a.shape[1]
    E = w_r.shape[2]
    spt = (D // 2) // LANES
    tm = ROW_TILE
    tpb = S // tm
    mod = lambda row: pl.BlockSpec((1, 1, D), lambda i: (row(i // tpb), 0, 0))
    lnspec = pl.BlockSpec((1, 1, D), lambda i: (ln_row, 0, 0))
    return pl.pallas_call(
        functools.partial(_proj_ln_router_kernel, alpha=alpha),
        grid=(T // tm,),
        in_specs=[
            pl.BlockSpec((tm, K), lambda i: (i, 0)),
            _layer_resident((K, D), lw),
            pl.BlockSpec((tm, D), lambda i: (i, 0)),
            mod(g1_row), lnspec, lnspec,
            mod(sh_row), mod(sc_row),
            _layer_resident((D, E), l), _layer_resident((1, E), l),
        ],
        out_specs=[
            pl.BlockSpec((tm, D), lambda i: (i, 0)),
            pl.BlockSpec((tm * spt, LANES), lambda i: (i, 0)),
            pl.BlockSpec((tm, LANES), lambda i: (i, 0)),
            pl.BlockSpec((1, E), lambda i: (0, 0)),
        ],
        out_shape=[
            jax.ShapeDtypeStruct((T, D), F32),
            jax.ShapeDtypeStruct((T * spt, LANES), I32),
            jax.ShapeDtypeStruct((T, LANES), I32),
            jax.ShapeDtypeStruct((1, E), I32),
        ],
        scratch_shapes=[pltpu.VMEM((1, E), F32)],
        compiler_params=_params(48, ("arbitrary",)),
        name="proj_ln_router",
    )(a, w, x, mods, ln[0], ln[1], mods, mods, w_r, b_r)


def _moe_up_kernel(te_ref, nv_ref, src_ref, xp_ref, wgu_ref, bgu_ref, act_ref,
                   g_scr, *, spt):
    tm, F = act_ref.shape
    i = pl.program_id(0)
    nt = pl.num_programs(0)
    nv = nv_ref[0]

    def gather(tile, slot):
        base = tile * tm
        for r in range(tm):
            src = pl.multiple_of(src_ref[base + r] * spt, spt)
            g_scr[slot, r * spt:(r + 1) * spt, :] = xp_ref[pl.ds(src, spt), :]

    @pl.when(i == 0)
    def _():
        gather(0, 0)

    @pl.when(i < nv)
    def _():
        slot = i % 2
        los, his = [], []
        for s in range(spt):
            lo, hi = _unpack_bf16_pair(g_scr[slot, pl.ds(s, tm, stride=spt), :])
            los.append(lo)
            his.append(hi)
        xt = jnp.concatenate(los + his, axis=1).astype(BF16)
        gather(jnp.minimum(i + 1, nt - 1), 1 - slot)
        h = jnp.dot(xt, wgu_ref[0, 0], preferred_element_type=F32) + bgu_ref[0, 0]
        g = jnp.minimum(h[:, :F], SWIGLU_LIMIT)
        u = jnp.clip(h[:, F:], -SWIGLU_LIMIT, SWIGLU_LIMIT)
        act_ref[...] = ((u + 1.0) * g * jax.nn.sigmoid(SWIGLU_ALPHA * g)).astype(BF16)

    @pl.when(i >= nv)
    def _():
        act_ref[...] = jnp.zeros_like(act_ref)


def _moe_up_call(tile_expert, n_valid, src_row, xp, w_gu, b_gu, l):
    L, E, D, F2 = w_gu.shape
    spt = (D // 2) // LANES
    tm = MOE_TILE
    nt = tile_expert.shape[0]
    grid_spec = pltpu.PrefetchScalarGridSpec(
        num_scalar_prefetch=3,
        grid=(nt,),
        in_specs=[
            pl.BlockSpec(xp.shape, lambda i, te, nv, src: (0, 0), pipeline_mode=pl.Buffered(1)),
            pl.BlockSpec((1, 1, D, F2), lambda i, te, nv, src: (l, te[i], 0, 0)),
            pl.BlockSpec((1, 1, 1, F2), lambda i, te, nv, src: (l, te[i], 0, 0)),
        ],
        out_specs=pl.BlockSpec((tm, F2 // 2), lambda i, te, nv, src: (i, 0)),
        scratch_shapes=[
            pltpu.VMEM((2, tm * spt, LANES), I32),
        ],
    )
    return pl.pallas_call(
        functools.partial(_moe_up_kernel, spt=spt),
        grid_spec=grid_spec,
        out_shape=jax.ShapeDtypeStruct((nt * tm, F2 // 2), BF16),
        compiler_params=_params(58, ("arbitrary",)),
        name="moe_up",
    )(tile_expert, n_valid, src_row, xp, w_gu, b_gu.reshape(L, E, 1, F2))


def _moe_down_kernel(te_ref, nv_ref, dst_ref, act_ref, gate_ref, wd_ref, bd_ref, o_hbm,
                     o_scr, y_scr, sem):
    tm = act_ref.shape[0]
    spt = y_scr.shape[1] // tm
    i = pl.program_id(0)
    last = pl.num_programs(0) - 1
    nv = nv_ref[0]

    @pl.when(i == 0)
    def _():
        o_scr[...] = jnp.zeros_like(o_scr)

    def matmul(slot):
        y = jnp.dot(act_ref[...], wd_ref[0, 0].astype(BF16), preferred_element_type=F32)
        gate = jnp.broadcast_to(gate_ref[0], (LANES, tm)).T
        for s in range(spt):
            cols = slice(s * LANES, (s + 1) * LANES)
            y_scr[slot, pl.ds(s, tm, stride=spt), :] = (y[:, cols] + bd_ref[0, 0, :, cols]) * gate

    def accumulate(tile, slot):
        base = tile * tm
        for c0 in range(0, tm, ACC_CHUNK):
            dsts = [pl.multiple_of(dst_ref[base + r] * spt, spt)
                    for r in range(c0, c0 + ACC_CHUNK)]
            sums = [o_scr[pl.ds(dsts[k], spt), :] + y_scr[slot, r * spt:(r + 1) * spt, :]
                    for k, r in enumerate(range(c0, c0 + ACC_CHUNK))]
            for k in range(ACC_CHUNK):
                o_scr[pl.ds(dsts[k], spt), :] = sums[k]

    @pl.when(jnp.logical_and(i == 0, i < nv))
    def _():
        matmul(0)

    @pl.when(jnp.logical_and(i >= 1, i < nv))
    def _():
        accumulate(i - 1, (i - 1) % 2)
        matmul(i % 2)

    @pl.when(jnp.logical_and(i >= 1, i == nv))
    def _():
        accumulate(i - 1, (i - 1) % 2)

    @pl.when(i == last)
    def _():
        out = pltpu.make_async_copy(o_scr, o_hbm, sem)
        out.start()
        out.wait()


def _moe_down_call(tile_expert, n_valid, dst_row, gate_sorted, act, w_down, b_down, l, half,
                   n_tokens):
    L, E, F, D = w_down.shape
    dh = D // 2
    spt = dh // LANES
    tm = MOE_TILE
    nt = tile_expert.shape[0]
    n_rows = (n_tokens + 1) * spt
    tile = lambda i: jnp.minimum(i, nt - 1)
    grid_spec = pltpu.PrefetchScalarGridSpec(
        num_scalar_prefetch=3,
        grid=(nt + 1,),
        in_specs=[
            pl.BlockSpec((tm, F), lambda i, te, nv, dst: (tile(i), 0)),
            pl.BlockSpec((1, 1, tm), lambda i, te, nv, dst: (tile(i), 0, 0)),
            pl.BlockSpec((1, 1, F, dh), lambda i, te, nv, dst: (l, te[tile(i)], 0, half)),
            pl.BlockSpec((1, 1, 1, dh), lambda i, te, nv, dst: (l, te[tile(i)], 0, half)),
        ],
        out_specs=pl.BlockSpec(memory_space=pl.ANY),
        scratch_shapes=[
            pltpu.VMEM((n_rows, LANES), F32),
            pltpu.VMEM((2, tm * spt, LANES), F32),
            pltpu.SemaphoreType.DMA(()),
        ],
    )
    return pl.pallas_call(
        _moe_down_kernel,
        grid_spec=grid_spec,
        out_shape=jax.ShapeDtypeStruct((n_rows, LANES), F32),
        compiler_params=_params(56, ("arbitrary",)),
        name="moe_down",
    )(tile_expert, n_valid, dst_row, act, gate_sorted.reshape(nt, 1, tm), w_down,
      b_down.reshape(L, E, 1, D))


def _moe_routing(meta, counts, n_tokens, n_experts, d_model):
    tm = MOE_TILE
    n_pairs = TOP_K_EXPERTS * n_tokens
    nt = n_pairs // tm + n_experts
    ei = meta[:, 0:TOP_K_EXPERTS]
    tw = lax.bitcast_convert_type(meta[:, TOP_K_EXPERTS:2 * TOP_K_EXPERTS], F32)
    rk = meta[:, 2 * TOP_K_EXPERTS:3 * TOP_K_EXPERTS]
    cnt = counts[0]
    tiles = (cnt + tm - 1) // tm
    tile_end = jnp.cumsum(tiles)
    group_start = (tile_end - tiles) * tm
    dest = (group_start[ei] + rk).reshape(-1)
    inv = jnp.full((nt * tm,), -1, I32).at[dest].set(jnp.arange(n_pairs, dtype=I32))
    live = inv >= 0
    pair = jnp.maximum(inv, 0)
    tok = pair // TOP_K_EXPERTS
    src_row = jnp.where(live, tok, 0)
    dst_row = jnp.where(live, tok, n_tokens)
    gate_sorted = jnp.where(live, tw.reshape(-1)[pair], 0.0)
    n_valid = tile_end[-1:].astype(I32)
    tid = jnp.minimum(jnp.arange(nt, dtype=I32), n_valid[0] - 1)
    tile_expert = jnp.sum((tile_end[None, :] <= tid[:, None]).astype(I32), axis=1)
    return tile_expert, n_valid, src_row, dst_row, gate_sorted


def _moe_res_ln_kernel(oa_ref, ob_ref, x1_ref, g2_ref, lg_ref, lb_ref, o_ref, *, alpha):
    tc = x1_ref.shape[0]
    spt = oa_ref.shape[0] // tc
    parts = [ref[pl.ds(s, tc, stride=spt), :] for ref in (oa_ref, ob_ref) for s in range(spt)]
    moe = jnp.concatenate(parts, axis=1)
    r = alpha * x1_ref[...] + (1.0 + g2_ref[0]) * moe
    o_ref[...] = _layer_norm(r, lg_ref[0], lb_ref[0])


def _moe_res_ln_call(o_a, o_b, x1, mods, g2_row, ln, ln_row, S, alpha):
    T, D = x1.shape
    spt = (D // 2) // LANES
    tc = ROW_TILE
    tpb = S // tc
    half = pl.BlockSpec((tc * spt, LANES), lambda i: (i, 0))
    lnspec = pl.BlockSpec((1, 1, D), lambda i: (ln_row, 0, 0))
    return pl.pallas_call(
        functools.partial(_moe_res_ln_kernel, alpha=alpha),
        grid=(T // tc,),
        in_specs=[half, half,
                  pl.BlockSpec((tc, D), lambda i: (i, 0)),
                  pl.BlockSpec((1, 1, D), lambda i: (g2_row(i // tpb), 0, 0)),
                  lnspec, lnspec],
        out_specs=pl.BlockSpec((tc, D), lambda i: (i, 0)),
        out_shape=jax.ShapeDtypeStruct((T, D), F32),
        compiler_params=_params(40, ("arbitrary",)),
        name="moe_res_ln",
    )(o_a, o_b, x1, mods, ln[0], ln[1])


def _q_proj_kernel(x_ref, sh_ref, sc_ref, wt_ref, o_ref, *, out_scale):
    h = (x_ref[...] * (1.0 + sc_ref[0]) + sh_ref[0]).astype(BF16)
    zt = lax.dot_general(wt_ref[0], h, (((1,), (1,)), ((), ())), preferred_element_type=F32)
    o_ref[0] = (zt * out_scale).astype(BF16)


def _q_proj_call(x, mods, sh_row, sc_row, w_t, l, S, out_scale):
    T, D = x.shape
    N = w_t.shape[1]
    tm = MOBA_BLOCK
    tpb = S // tm
    mod = lambda row: pl.BlockSpec((1, 1, D), lambda i: (row(i // tpb), 0, 0))
    return pl.pallas_call(
        functools.partial(_q_proj_kernel, out_scale=out_scale),
        grid=(T // tm,),
        in_specs=[pl.BlockSpec((tm, D), lambda i: (i, 0)), mod(sh_row), mod(sc_row),
                  _layer_resident((N, D), l)],
        out_specs=pl.BlockSpec((1, N, tm), lambda i: (i, 0, 0)),
        out_shape=jax.ShapeDtypeStruct((T // tm, N, tm), BF16),
        compiler_params=_params(48, ("arbitrary",)),
        name="q_proj",
    )(x, mods, mods, w_t)


def _kv_proj_kernel(x_ref, sh_ref, sc_ref, wk_ref, wvt_ref, k_ref, vt_ref, km_ref):
    h = (x_ref[...] * (1.0 + sc_ref[0]) + sh_ref[0]).astype(BF16)
    zk = jnp.dot(h, wk_ref[...], preferred_element_type=F32)
    k_ref[...] = zk.astype(BF16)
    km_ref[0] = jnp.mean(zk, axis=0, keepdims=True)
    zvt = lax.dot_general(wvt_ref[...], h, (((1,), (1,)), ((), ())),
                          preferred_element_type=F32)
    vt_ref[0] = zvt.astype(BF16)


def _kv_proj_call(x, mods, sh_row, sc_row, w_k, w_vt, S):
    T, D = x.shape
    HD = w_k.shape[1]
    tm = MOBA_BLOCK
    tpb = S // tm
    mod = lambda row: pl.BlockSpec((1, 1, D), lambda i: (row(i // tpb), 0, 0))
    return pl.pallas_call(
        _kv_proj_kernel,
        grid=(T // tm,),
        in_specs=[pl.BlockSpec((tm, D), lambda i: (i, 0)), mod(sh_row), mod(sc_row),
                  _resident((D, HD)), _resident((HD, D))],
        out_specs=[pl.BlockSpec((tm, HD), lambda i: (i, 0)),
                   pl.BlockSpec((1, HD, tm), lambda i: (i, 0, 0)),
                   pl.BlockSpec((1, 1, HD), lambda i: (i, 0, 0))],
        out_shape=[jax.ShapeDtypeStruct((T, HD), BF16),
                   jax.ShapeDtypeStruct((T // tm, HD, tm), BF16),
                   jax.ShapeDtypeStruct((T // tm, 1, HD), F32)],
        compiler_params=_params(48, ("arbitrary",)),
        name="kv_proj",
    )(x, mods, mods, w_k, w_vt)


def _moba_kernel(qt_ref, k_ref, vt_ref, km_ref, o_ref, m_scr, l_scr, acc_scr, s_scr, *, hps):
    bs = qt_ref.shape[2]
    nb = km_ref.shape[1]
    qb = pl.program_id(2)
    start = pl.multiple_of(qb * bs, bs)
    kidx = lax.broadcasted_iota(I32, (bs, bs), 0)
    qidx = lax.broadcasted_iota(I32, (bs, bs), 1)
    blk = lax.broadcasted_iota(I32, (nb, bs), 0)
    valid = blk < qb
    heads = [slice(h * HEAD_DIM, (h + 1) * HEAD_DIM) for h in range(hps)]

    biases = []
    for cs in heads:
        qt = qt_ref[0, cs, :]
        km = km_ref[0, :, cs]
        km_hi = km.astype(BF16)
        km_lo = (km - km_hi.astype(F32)).astype(BF16)
        gate = (jnp.dot(km_hi, qt, preferred_element_type=F32)
                + jnp.dot(km_lo, qt, preferred_element_type=F32))
        gate = jnp.where(valid, gate, -jnp.inf)
        beaten = jnp.zeros((nb, bs), F32)
        for jp in range(nb):
            row = gate[jp:jp + 1, :]
            wins = jnp.logical_or(row > gate, jnp.logical_and(row == gate, jp < blk))
            beaten = beaten + jnp.where(wins, 1.0, 0.0)
        biases.append(jnp.where(jnp.logical_and(valid, beaten < MOBA_TOPK), 0.0, MASK_VALUE))

    def scores(rows, slot):
        for h, cs in enumerate(heads):
            s_scr[slot, h] = jnp.dot(k_ref[rows, cs], qt_ref[0, cs, :],
                                     preferred_element_type=F32)

    own_slot = 1
    scores(pl.ds(start, bs), own_slot)
    scores(slice(0, bs), 0)

    for h, cs in enumerate(heads):
        s = jnp.where(kidx <= qidx, s_scr[own_slot, h], MASK_VALUE)
        m = jnp.max(s, axis=0, keepdims=True)
        p = jnp.exp2(s - m)
        m_scr[h] = m
        l_scr[h] = jnp.sum(p, axis=0, keepdims=True)
        acc_scr[h] = jnp.dot(vt_ref[qb, cs, :], p.astype(BF16), preferred_element_type=F32)

    for j in range(nb - 1):
        @pl.when(j < qb)
        def _(j=j):
            if j + 1 < nb - 1:
                scores(slice((j + 1) * bs, (j + 2) * bs), (j + 1) % 2)
            for h, cs in enumerate(heads):
                s = s_scr[j % 2, h] + biases[h][j:j + 1, :]
                m_old = m_scr[h]
                m_new = jnp.maximum(m_old, jnp.max(s, axis=0, keepdims=True))
                a = jnp.exp2(m_old - m_new)
                p = jnp.exp2(s - m_new)
                l_scr[h] = a * l_scr[h] + jnp.sum(p, axis=0, keepdims=True)
                acc_scr[h] = a * acc_scr[h] + jnp.dot(
                    vt_ref[j, cs, :], p.astype(BF16), preferred_element_type=F32)
                m_scr[h] = m_new

    for h, cs in enumerate(heads):
        o_ref[:, cs] = (acc_scr[h] * (1.0 / l_scr[h])).T.astype(BF16)


def _moba_call(qt, k, vt, kmeans, B, S, hps):
    T, HD = k.shape
    H = HD // HEAD_DIM
    bs = MOBA_BLOCK
    nb = S // bs
    hw = hps * HEAD_DIM
    return pl.pallas_call(
        functools.partial(_moba_kernel, hps=hps),
        grid=(B, H // hps, nb),
        in_specs=[
            pl.BlockSpec((1, hw, bs), lambda b, g, i: (b * nb + i, g, 0)),
            pl.BlockSpec((S, hw), lambda b, g, i: (b, g)),
            pl.BlockSpec((nb, hw, bs), lambda b, g, i: (b, g, 0)),
            pl.BlockSpec((1, nb, hw), lambda b, g, i: (b, 0, g)),
        ],
        out_specs=pl.BlockSpec((bs, hw), lambda b, g, i: (b * nb + i, g)),
        out_shape=jax.ShapeDtypeStruct((T, HD), BF16),
        scratch_shapes=[pltpu.VMEM((hps, 1, bs), F32), pltpu.VMEM((hps, 1, bs), F32),
                        pltpu.VMEM((hps, HEAD_DIM, bs), F32),
                        pltpu.VMEM((2, hps, bs, bs), F32)],
        compiler_params=_params(48, ("arbitrary", "arbitrary", "arbitrary")),
        name="moba_attention",
    )(qt, k, vt, kmeans.reshape(B, nb, HD))


def kernel(x, c, ada_w, ada_b, ln_g, ln_b, gm_w_in, gm_b_in, gm_lnv_g, gm_lnv_b, gm_w_s,
           gm_b_s, gm_w_out, kv_ada_w, kv_ada_b, w_kv, attn_w_q, attn_w_out, moe_w_router,
           moe_b_router, moe_w_gu, moe_b_gu, moe_w_down, moe_b_down):
    B, S, D = x.shape
    T = B * S
    depth = ada_w.shape[0]
    n_a = gm_w_in.shape[0]
    E = moe_w_router.shape[2]
    W = gm_w_in.shape[2] // 2
    C = gm_w_s.shape[2]
    H = attn_w_q.shape[2] // HEAD_DIM
    hps = 4 if H % 4 == 0 else 1
    assert B <= MODS_PAD_ROWS and S % MOBA_BLOCK == 0 and S % ROW_TILE == 0
    assert ROW_TILE % C == 0 and W // gm_w_s.shape[1] == LANES
    assert (D // 2) % LANES == 0 and (TOP_K_EXPERTS * T) % MOE_TILE == 0
    alpha = (2.0 * depth) ** 0.25

    c_pad = jnp.zeros((MODS_PAD_ROWS, D), F32).at[:B].set(c)
    n_mod = ada_w.shape[2] // D
    mods = _mods_call(c_pad, ada_w, ada_b).reshape(depth * MODS_PAD_ROWS * n_mod, 1, D)
    kv_mods = _mods_call(c_pad, kv_ada_w[None], kv_ada_b[None]).reshape(
        MODS_PAD_ROWS * 2, 1, D)

    def mod_row(l, which):
        return lambda b: (l * MODS_PAD_ROWS + b) * n_mod + which

    ln_g3 = ln_g.reshape(depth * 2, 1, D)
    ln_b3 = ln_b.reshape(depth * 2, 1, D)
    causal = jnp.tril(jnp.ones((C, C), F32))
    gm_w_in_b = gm_w_in.astype(BF16)
    gm_w_out_b = gm_w_out.astype(BF16)
    gm_ws_b = (gm_w_s * causal).astype(BF16)
    gm_bs_t = jnp.swapaxes(gm_b_s, 1, 2)
    attn_w_qt_b = jnp.swapaxes(attn_w_q, 1, 2).astype(BF16)
    attn_w_out_b = attn_w_out.astype(BF16)
    w_r_b = moe_w_router.astype(BF16)
    w_gu_b = moe_w_gu.astype(BF16)

    xt = x.reshape(T, D)
    k = vt = kmeans = None
    for l in range(depth):
        if l < n_a:
            y = _gmlp_in_call(
                xt, mods, mod_row(l, 0), mod_row(l, 1), gm_w_in_b, gm_b_in[:, None],
                gm_lnv_g[:, None], gm_lnv_b[:, None], gm_ws_b, gm_bs_t, l, S)
            w_o, lw = gm_w_out_b, l
        else:
            lw = l - n_a
            qt = _q_proj_call(xt, mods, mod_row(l, 0), mod_row(l, 1), attn_w_qt_b, lw, S,
                              HEAD_DIM ** -0.5 * LOG2_E)
            y = _moba_call(qt, k, vt, kmeans, B, S, hps)
            w_o = attn_w_out_b
        x1, xp, meta, counts = _proj_ln_router_call(
            y, w_o, lw, xt, mods, mod_row(l, 2), mod_row(l, 3), mod_row(l, 4),
            (ln_g3, ln_b3), 2 * l, w_r_b, moe_b_router[:, None], l, S, alpha)
        tile_expert, n_valid, src_row, dst_row, gate_sorted = _moe_routing(
            meta, counts, T, E, D)
        act = _moe_up_call(tile_expert, n_valid, src_row, xp, w_gu_b, moe_b_gu, l)
        halves = [_moe_down_call(tile_expert, n_valid, dst_row, gate_sorted, act, moe_w_down,
                                 moe_b_down, l, half, T) for half in range(2)]
        xt = _moe_res_ln_call(halves[0], halves[1], x1, mods, mod_row(l, 5), (ln_g3, ln_b3),
                              2 * l + 1, S, alpha)
        if l == n_a - 1:
            hd = w_kv.shape[1] // 2
            k, vt, kmeans = _kv_proj_call(
                xt, kv_mods, lambda b: b * 2, lambda b: b * 2 + 1,
                w_kv[:, :hd].astype(BF16), w_kv[:, hd:].T.astype(BF16), S)
    return xt.reshape(B, S, D)
```

```python
import functools

import jax
import jax.numpy as jnp
from jax import lax
from jax.experimental import pallas as pl
from jax.experimental.pallas import tpu as pltpu

HEAD_DIM = 128
MOBA_BLOCK = 256
MOBA_TOPK = 3
TOP_K_EXPERTS = 4
SWIGLU_LIMIT = 7.0
SWIGLU_ALPHA = 1.702
LN_EPS = 1e-5

LANES = 128
SUBLANES = 8

ROW_TILE = 256
MOE_TILE = 512
ACC_CHUNK = 16
ROUTE_UNROLL = 8
MODS_PAD_ROWS = SUBLANES
MASK_VALUE = -1e30
LOG2_E = 1.4426950408889634
HI16 = -65536

F32 = jnp.float32
BF16 = jnp.bfloat16
I32 = jnp.int32


def _params(vmem_mb, semantics):
    return pltpu.CompilerParams(dimension_semantics=semantics,
                                vmem_limit_bytes=vmem_mb * 1024 * 1024)


def _resident(shape):
    nd = len(shape)
    return pl.BlockSpec(shape, lambda *_: (0,) * nd, pipeline_mode=pl.Buffered(1))


def _layer_resident(tail, l):
    nd = len(tail)
    return pl.BlockSpec((1,) + tuple(tail), lambda *_: (l,) + (0,) * nd,
                        pipeline_mode=pl.Buffered(1))


def _layer_norm(r, g, b):
    mu = jnp.mean(r, axis=-1, keepdims=True)
    d = r - mu
    var = jnp.mean(d * d, axis=-1, keepdims=True)
    return d * lax.rsqrt(var + LN_EPS) * g + b


def _pack_bf16_pair(lo, hi):
    lo = lax.bitcast_convert_type(lo.astype(BF16).astype(F32), I32)
    hi = lax.bitcast_convert_type(hi.astype(BF16).astype(F32), I32)
    return lax.shift_right_logical(lo, 16) | (hi & HI16)


def _unpack_bf16_pair(w):
    lo = lax.bitcast_convert_type(lax.shift_left(w, 16), F32)
    hi = lax.bitcast_convert_type(w & HI16, F32)
    return lo, hi


def _mods_kernel(c_ref, w_ref, b_ref, o_ref):
    c = c_ref[...]
    ca = (c * jax.nn.sigmoid(c)).astype(BF16)
    w = w_ref[0].astype(BF16)
    o_ref[0] = jnp.dot(ca, w, preferred_element_type=F32) + b_ref[0]


def _mods_call(c_pad, w, b, tn=1024):
    L, D, N = w.shape
    return pl.pallas_call(
        _mods_kernel,
        grid=(L, N // tn),
        in_specs=[
            pl.BlockSpec((MODS_PAD_ROWS, D), lambda l, j: (0, 0)),
            pl.BlockSpec((1, D, tn), lambda l, j: (l, 0, j)),
            pl.BlockSpec((1, 1, tn), lambda l, j: (l, 0, j)),
        ],
        out_specs=pl.BlockSpec((1, MODS_PAD_ROWS, tn), lambda l, j: (l, 0, j)),
        out_shape=jax.ShapeDtypeStruct((L, MODS_PAD_ROWS, N), F32),
        compiler_params=_params(40, ("arbitrary", "arbitrary")),
        name="adaln_mods",
    )(c_pad, w, b.reshape(L, 1, N))


def _gmlp_in_kernel(x_ref, sh_ref, sc_ref, w_ref, b_ref, lg_ref, lb_ref, ws_ref, bs_ref,
                    y_ref, u_scr, v_scr, vn_scr, *, ncol):
    tm, D = x_ref.shape
    W = y_ref.shape[1]
    _, G, C, _ = ws_ref.shape
    gd = W // G
    h = (x_ref[...] * (1.0 + sc_ref[0]) + sh_ref[0]).astype(BF16)
    for j in range(2 * W // ncol):
        z = jnp.dot(h, w_ref[0, :, j * ncol:(j + 1) * ncol], preferred_element_type=F32)
        z = jax.nn.gelu(z + b_ref[0, :, j * ncol:(j + 1) * ncol], approximate=True)
        if j * ncol < W:
            u_scr[:, j * ncol:(j + 1) * ncol] = z
        else:
            v_scr[:, j * ncol - W:(j + 1) * ncol - W] = z
    vn_scr[...] = _layer_norm(v_scr[...], lg_ref[0], lb_ref[0]).astype(BF16)
    for n in range(tm // C):
        rows = slice(n * C, (n + 1) * C)
        for g in range(G):
            cols = slice(g * gd, (g + 1) * gd)
            sv = jnp.dot(ws_ref[0, g], vn_scr[rows, cols], preferred_element_type=F32)
            sv = sv + bs_ref[0, :, g:g + 1]
            y_ref[rows, cols] = (u_scr[rows, cols] * sv).astype(BF16)


def _gmlp_in_call(x, mods, sh_row, sc_row, w_in, b_in, lnv_g, lnv_b, ws, bs_t, l, S):
    T, D = x.shape
    W = w_in.shape[2] // 2
    _, G, C, _ = ws.shape
    tm = ROW_TILE
    tpb = S // tm
    mod = lambda row: pl.BlockSpec((1, 1, D), lambda i: (row(i // tpb), 0, 0))
    return pl.pallas_call(
        functools.partial(_gmlp_in_kernel, ncol=512),
        grid=(T // tm,),
        in_specs=[
            pl.BlockSpec((tm, D), lambda i: (i, 0)),
            mod(sh_row), mod(sc_row),
            _layer_resident((D, 2 * W), l),
            _layer_resident((1, 2 * W), l),
            _layer_resident((1, W), l), _layer_resident((1, W), l),
            _layer_resident((G, C, C), l),
            _layer_resident((C, G), l),
        ],
        out_specs=pl.BlockSpec((tm, W), lambda i: (i, 0)),
        out_shape=jax.ShapeDtypeStruct((T, W), BF16),
        scratch_shapes=[pltpu.VMEM((tm, W), F32), pltpu.VMEM((tm, W), F32),
                        pltpu.VMEM((tm, W), BF16)],
        compiler_params=_params(48, ("arbitrary",)),
        name="gmlp_in",
    )(x, mods, mods, w_in, b_in, lnv_g, lnv_b, ws, bs_t)


def _proj_ln_router_kernel(a_ref, w_ref, x_ref, g1_ref, lg_ref, lb_ref, sh_ref, sc_ref,
                           wr_ref, br_ref,
                           x1_ref, xp_ref, meta_ref, cnt_ref, carry, *, alpha):
    tm, D = x_ref.shape
    E = wr_ref.shape[2]
    spt = xp_ref.shape[0] // tm
    i = pl.program_id(0)

    @pl.when(i == 0)
    def _():
        carry[...] = jnp.zeros_like(carry)

    h = jnp.dot(a_ref[...], w_ref[0], preferred_element_type=F32)
    x1 = _layer_norm(alpha * x_ref[...] + (1.0 + g1_ref[0]) * h, lg_ref[0], lb_ref[0])
    x1_ref[...] = x1
    xm = x1 * (1.0 + sc_ref[0]) + sh_ref[0]
    pk = _pack_bf16_pair(xm[:, :D // 2], xm[:, D // 2:])
    for s in range(spt):
        xp_ref[pl.ds(s, tm, stride=spt), :] = pk[:, s * LANES:(s + 1) * LANES]

    logits = jnp.dot(xm.astype(BF16), wr_ref[0], preferred_element_type=F32) + br_ref[0]
    lane = lax.broadcasted_iota(I32, (tm, E), 1).astype(F32)
    onehots, vals, idxs = [], [], []
    cur = logits
    for _k in range(TOP_K_EXPERTS):
        m = jnp.max(cur, axis=-1, keepdims=True)
        idx = jnp.min(jnp.where(cur == m, lane, float(E)), axis=-1, keepdims=True)
        oh = lane == idx
        cur = jnp.where(oh, -jnp.inf, cur)
        onehots.append(jnp.where(oh, 1.0, 0.0))
        vals.append(m)
        idxs.append(idx.astype(I32))
    exps = [jnp.exp(v - vals[0]) for v in vals]
    denom = exps[0]
    for e in exps[1:]:
        denom = denom + e
    oh_all = onehots[0]
    for oh in onehots[1:]:
        oh_all = oh_all + oh
    r_i = lax.broadcasted_iota(I32, (tm, tm), 0)
    c_i = lax.broadcasted_iota(I32, (tm, tm), 1)
    tri = jnp.where(r_i > c_i, 1.0, 0.0).astype(BF16)
    before = jnp.dot(tri, oh_all.astype(BF16), preferred_element_type=F32) + carry[...]
    mlane = lax.broadcasted_iota(I32, (tm, LANES), 1)
    meta = jnp.zeros((tm, LANES), I32)
    for k in range(TOP_K_EXPERTS):
        rank = jnp.sum(onehots[k] * before, axis=-1, keepdims=True).astype(I32)
        wk = lax.bitcast_convert_type(exps[k] / denom, I32)
        meta = jnp.where(mlane == k, idxs[k], meta)
        meta = jnp.where(mlane == TOP_K_EXPERTS + k, wk, meta)
        meta = jnp.where(mlane == 2 * TOP_K_EXPERTS + k, rank, meta)
    meta_ref[...] = meta
    carry[...] = carry[...] + jnp.sum(oh_all, axis=0, keepdims=True)
    cnt_ref[...] = carry[...].astype(I32)


def _proj_ln_router_call(a, w, lw, x, mods, g1_row, sh_row, sc_row, ln, ln_row, w_r, b_r, l,
                         S, alpha):
    T, D = x.shape
    K = a.shape[1]
    E = w_r.shape[2]
    spt = (D // 2) // LANES
    tm = ROW_TILE
    tpb = S // tm
    mod = lambda row: pl.BlockSpec((1, 1, D), lambda i: (row(i // tpb), 0, 0))
    lnspec = pl.BlockSpec((1, 1, D), lambda i: (ln_row, 0, 0))
    return pl.pallas_call(
        functools.partial(_proj_ln_router_kernel, alpha=alpha),
        grid=(T // tm,),
        in_specs=[
            pl.BlockSpec((tm, K), lambda i: (i, 0)),
            _layer_resident((K, D), lw),
            pl.BlockSpec((tm, D), lambda i: (i, 0)),
            mod(g1_row), lnspec, lnspec,
            mod(sh_row), mod(sc_row),
            _layer_resident((D, E), l), _layer_resident((1, E), l),
        ],
        out_specs=[
            pl.BlockSpec((tm, D), lambda i: (i, 0)),
            pl.BlockSpec((tm * spt, LANES), lambda i: (i, 0)),
            pl.BlockSpec((tm, LANES), lambda i: (i, 0)),
            pl.BlockSpec((1, E), lambda i: (0, 0)),
        ],
        out_shape=[
            jax.ShapeDtypeStruct((T, D), F32),
            jax.ShapeDtypeStruct((T * spt, LANES), I32),
            jax.ShapeDtypeStruct((T, LANES), I32),
            jax.ShapeDtypeStruct((1, E), I32),
        ],
        scratch_shapes=[pltpu.VMEM((1, E), F32)],
        compiler_params=_params(48, ("arbitrary",)),
        name="proj_ln_router",
    )(a, w, x, mods, ln[0], ln[1], mods, mods, w_r, b_r)


def _moe_up_kernel(te_ref, nv_ref, src_ref, xp_ref, wgu_ref, bgu_ref, act_ref,
                   g_scr, *, spt):
    tm, F = act_ref.shape
    i = pl.program_id(0)
    nt = pl.num_programs(0)
    nv = nv_ref[0]

    def gather(tile, slot):
        base = tile * tm
        last_tok = xp_ref.shape[0] // spt - 1
        for r in range(tm):
            src = pl.multiple_of(jnp.minimum(src_ref[base + r], last_tok) * spt, spt)
            g_scr[slot, r * spt:(r + 1) * spt, :] = xp_ref[pl.ds(src, spt), :]

    @pl.when(i == 0)
    def _():
        gather(0, 0)

    @pl.when(i < nv)
    def _():
        slot = i % 2
        los, his = [], []
        for s in range(spt):
            lo, hi = _unpack_bf16_pair(g_scr[slot, pl.ds(s, tm, stride=spt), :])
            los.append(lo)
            his.append(hi)
        xt = jnp.concatenate(los + his, axis=1).astype(BF16)
        gather(jnp.minimum(i + 1, nt - 1), 1 - slot)
        h = jnp.dot(xt, wgu_ref[0, 0], preferred_element_type=F32) + bgu_ref[0, 0]
        g = jnp.minimum(h[:, :F], SWIGLU_LIMIT)
        u = jnp.clip(h[:, F:], -SWIGLU_LIMIT, SWIGLU_LIMIT)
        act_ref[...] = ((u + 1.0) * g * jax.nn.sigmoid(SWIGLU_ALPHA * g)).astype(BF16)

    @pl.when(i >= nv)
    def _():
        act_ref[...] = jnp.zeros_like(act_ref)


def _moe_up_call(tile_expert, n_valid, src_row, xp, w_gu, b_gu, l):
    L, E, D, F2 = w_gu.shape
    spt = (D // 2) // LANES
    tm = MOE_TILE
    nt = tile_expert.shape[0]
    grid_spec = pltpu.PrefetchScalarGridSpec(
        num_scalar_prefetch=3,
        grid=(nt,),
        in_specs=[
            pl.BlockSpec(xp.shape, lambda i, te, nv, src: (0, 0), pipeline_mode=pl.Buffered(1)),
            pl.BlockSpec((1, 1, D, F2), lambda i, te, nv, src: (l, te[i], 0, 0)),
            pl.BlockSpec((1, 1, 1, F2), lambda i, te, nv, src: (l, te[i], 0, 0)),
        ],
        out_specs=pl.BlockSpec((tm, F2 // 2), lambda i, te, nv, src: (i, 0)),
        scratch_shapes=[
            pltpu.VMEM((2, tm * spt, LANES), I32),
        ],
    )
    return pl.pallas_call(
        functools.partial(_moe_up_kernel, spt=spt),
        grid_spec=grid_spec,
        out_shape=jax.ShapeDtypeStruct((nt * tm, F2 // 2), BF16),
        compiler_params=_params(58, ("arbitrary",)),
        name="moe_up",
    )(tile_expert, n_valid, src_row, xp, w_gu, b_gu.reshape(L, E, 1, F2))


def _moe_down_kernel(te_ref, nv_ref, dst_ref, act_ref, gate_ref, wd_ref, bd_ref, o_hbm,
                     o_scr, y_scr, sem):
    tm = act_ref.shape[0]
    spt = y_scr.shape[1] // tm
    i = pl.program_id(0)
    last = pl.num_programs(0) - 1
    nv = nv_ref[0]

    @pl.when(i == 0)
    def _():
        o_scr[...] = jnp.zeros_like(o_scr)

    def matmul(slot):
        y = jnp.dot(act_ref[...], wd_ref[0, 0].astype(BF16), preferred_element_type=F32)
        gate = jnp.broadcast_to(gate_ref[0], (LANES, tm)).T
        for s in range(spt):
            cols = slice(s * LANES, (s + 1) * LANES)
            y_scr[slot, pl.ds(s, tm, stride=spt), :] = (y[:, cols] + bd_ref[0, 0, :, cols]) * gate

    def accumulate(tile, slot):
        base = tile * tm
        for c0 in range(0, tm, ACC_CHUNK):
            dsts = [pl.multiple_of(dst_ref[base + r] * spt, spt)
                    for r in range(c0, c0 + ACC_CHUNK)]
            sums = [o_scr[pl.ds(dsts[k], spt), :] + y_scr[slot, r * spt:(r + 1) * spt, :]
                    for k, r in enumerate(range(c0, c0 + ACC_CHUNK))]
            for k in range(ACC_CHUNK):
                o_scr[pl.ds(dsts[k], spt), :] = sums[k]

    @pl.when(jnp.logical_and(i == 0, i < nv))
    def _():
        matmul(0)

    @pl.when(jnp.logical_and(i >= 1, i < nv))
    def _():
        accumulate(i - 1, (i - 1) % 2)
        matmul(i % 2)

    @pl.when(jnp.logical_and(i >= 1, i == nv))
    def _():
        accumulate(i - 1, (i - 1) % 2)

    @pl.when(i == last)
    def _():
        out = pltpu.make_async_copy(o_scr, o_hbm, sem)
        out.start()
        out.wait()


def _moe_down_call(tile_expert, n_valid, dst_row, gate_sorted, act, w_down, b_down, l, half,
                   n_tokens):
    L, E, F, D = w_down.shape
    dh = D // 2
    spt = dh // LANES
    tm = MOE_TILE
    nt = tile_expert.shape[0]
    n_rows = (n_tokens + 1) * spt
    tile = lambda i: jnp.minimum(i, nt - 1)
    grid_spec = pltpu.PrefetchScalarGridSpec(
        num_scalar_prefetch=3,
        grid=(nt + 1,),
        in_specs=[
            pl.BlockSpec((tm, F), lambda i, te, nv, dst: (tile(i), 0)),
            pl.BlockSpec((1, 1, tm), lambda i, te, nv, dst: (tile(i), 0, 0)),
            pl.BlockSpec((1, 1, F, dh), lambda i, te, nv, dst: (l, te[tile(i)], 0, half)),
            pl.BlockSpec((1, 1, 1, dh), lambda i, te, nv, dst: (l, te[tile(i)], 0, half)),
        ],
        out_specs=pl.BlockSpec(memory_space=pl.ANY),
        scratch_shapes=[
            pltpu.VMEM((n_rows, LANES), F32),
            pltpu.VMEM((2, tm * spt, LANES), F32),
            pltpu.SemaphoreType.DMA(()),
        ],
    )
    return pl.pallas_call(
        _moe_down_kernel,
        grid_spec=grid_spec,
        out_shape=jax.ShapeDtypeStruct((n_rows, LANES), F32),
        compiler_params=_params(56, ("arbitrary",)),
        name="moe_down",
    )(tile_expert, n_valid, dst_row, act, gate_sorted.reshape(nt, 1, tm), w_down,
      b_down.reshape(L, E, 1, D))


def _route_invert_kernel(dest_ref, gate_ref, tok_fill, gate_fill, tok_ref, gsort_ref, sem):
    n_pairs = dest_ref.shape[0]
    fills = [pltpu.make_async_copy(tok_fill, tok_ref, sem.at[0]),
             pltpu.make_async_copy(gate_fill, gsort_ref, sem.at[1])]
    for f in fills:
        f.start()
    for f in fills:
        f.wait()

    per_step = ROUTE_UNROLL // TOP_K_EXPERTS

    def place(b, c):
        for q in range(ROUTE_UNROLL):
            p = b * ROUTE_UNROLL + q
            d = dest_ref[p]
            tok_ref[d] = b * per_step + q // TOP_K_EXPERTS
            gsort_ref[d] = gate_ref[p]
        return c

    lax.fori_loop(0, n_pairs // ROUTE_UNROLL, place, 0)


def _route_invert_call(dest, gate, n_slots, n_tokens):
    smem = pl.BlockSpec(memory_space=pltpu.SMEM)
    hbm = pl.BlockSpec(memory_space=pl.ANY)
    return pl.pallas_call(
        _route_invert_kernel,
        in_specs=[smem, smem, hbm, hbm],
        out_specs=[smem, smem],
        out_shape=[jax.ShapeDtypeStruct((n_slots,), I32),
                   jax.ShapeDtypeStruct((n_slots,), F32)],
        scratch_shapes=[pltpu.SemaphoreType.DMA((2,))],
        name="route_invert",
    )(dest, gate, jnp.full((n_slots,), n_tokens, I32), jnp.zeros((n_slots,), F32))


def _moe_routing(meta, counts, n_tokens, n_experts):
    tm = MOE_TILE
    n_pairs = TOP_K_EXPERTS * n_tokens
    nt = n_pairs // tm + n_experts
    ei = meta[:, 0:TOP_K_EXPERTS]
    tw = lax.bitcast_convert_type(meta[:, TOP_K_EXPERTS:2 * TOP_K_EXPERTS], F32)
    rk = meta[:, 2 * TOP_K_EXPERTS:3 * TOP_K_EXPERTS]
    cnt = counts[0]
    tiles = (cnt + tm - 1) // tm
    tile_end = jnp.cumsum(tiles)
    group_start = (tile_end - tiles) * tm
    dest = (group_start[ei] + rk).reshape(-1)
    tok_sorted, gate_sorted = _route_invert_call(dest, tw.reshape(-1), nt * tm, n_tokens)
    n_valid = tile_end[-1:].astype(I32)
    tid = jnp.minimum(jnp.arange(nt, dtype=I32), n_valid[0] - 1)
    tile_expert = jnp.sum((tile_end[None, :] <= tid[:, None]).astype(I32), axis=1)
    return tile_expert, n_valid, tok_sorted, gate_sorted


def _moe_res_ln_kernel(oa_ref, ob_ref, x1_ref, g2_ref, lg_ref, lb_ref, o_ref, *, alpha):
    tc = x1_ref.shape[0]
    spt = oa_ref.shape[0] // tc
    parts = [ref[pl.ds(s, tc, stride=spt), :] for ref in (oa_ref, ob_ref) for s in range(spt)]
    moe = jnp.concatenate(parts, axis=1)
    r = alpha * x1_ref[...] + (1.0 + g2_ref[0]) * moe
    o_ref[...] = _layer_norm(r, lg_ref[0], lb_ref[0])


def _moe_res_ln_call(o_a, o_b, x1, mods, g2_row, ln, ln_row, S, alpha):
    T, D = x1.shape
    spt = (D // 2) // LANES
    tc = ROW_TILE
    tpb = S // tc
    half = pl.BlockSpec((tc * spt, LANES), lambda i: (i, 0))
    lnspec = pl.BlockSpec((1, 1, D), lambda i: (ln_row, 0, 0))
    return pl.pallas_call(
        functools.partial(_moe_res_ln_kernel, alpha=alpha),
        grid=(T // tc,),
        in_specs=[half, half,
                  pl.BlockSpec((tc, D), lambda i: (i, 0)),
                  pl.BlockSpec((1, 1, D), lambda i: (g2_row(i // tpb), 0, 0)),
                  lnspec, lnspec],
        out_specs=pl.BlockSpec((tc, D), lambda i: (i, 0)),
        out_shape=jax.ShapeDtypeStruct((T, D), F32),
        compiler_params=_params(40, ("arbitrary",)),
        name="moe_res_ln",
    )(o_a, o_b, x1, mods, ln[0], ln[1])


def _q_proj_kernel(x_ref, sh_ref, sc_ref, wt_ref, o_ref, *, out_scale):
    h = (x_ref[...] * (1.0 + sc_ref[0]) + sh_ref[0]).astype(BF16)
    zt = lax.dot_general(wt_ref[0], h, (((1,), (1,)), ((), ())), preferred_element_type=F32)
    o_ref[0] = (zt * out_scale).astype(BF16)


def _q_proj_call(x, mods, sh_row, sc_row, w_t, l, S, out_scale):
    T, D = x.shape
    N = w_t.shape[1]
    tm = MOBA_BLOCK
    tpb = S // tm
    mod = lambda row: pl.BlockSpec((1, 1, D), lambda i: (row(i // tpb), 0, 0))
    return pl.pallas_call(
        functools.partial(_q_proj_kernel, out_scale=out_scale),
        grid=(T // tm,),
        in_specs=[pl.BlockSpec((tm, D), lambda i: (i, 0)), mod(sh_row), mod(sc_row),
                  _layer_resident((N, D), l)],
        out_specs=pl.BlockSpec((1, N, tm), lambda i: (i, 0, 0)),
        out_shape=jax.ShapeDtypeStruct((T // tm, N, tm), BF16),
        compiler_params=_params(48, ("arbitrary",)),
        name="q_proj",
    )(x, mods, mods, w_t)


def _kv_proj_kernel(x_ref, sh_ref, sc_ref, wk_ref, wvt_ref, k_ref, vt_ref, km_ref):
    h = (x_ref[...] * (1.0 + sc_ref[0]) + sh_ref[0]).astype(BF16)
    zk = jnp.dot(h, wk_ref[...], preferred_element_type=F32)
    k_ref[...] = zk.astype(BF16)
    km_ref[0] = jnp.mean(zk, axis=0, keepdims=True)
    zvt = lax.dot_general(wvt_ref[...], h, (((1,), (1,)), ((), ())),
                          preferred_element_type=F32)
    vt_ref[0] = zvt.astype(BF16)


def _kv_proj_call(x, mods, sh_row, sc_row, w_k, w_vt, S):
    T, D = x.shape
    HD = w_k.shape[1]
    tm = MOBA_BLOCK
    tpb = S // tm
    mod = lambda row: pl.BlockSpec((1, 1, D), lambda i: (row(i // tpb), 0, 0))
    return pl.pallas_call(
        _kv_proj_kernel,
        grid=(T // tm,),
        in_specs=[pl.BlockSpec((tm, D), lambda i: (i, 0)), mod(sh_row), mod(sc_row),
                  _resident((D, HD)), _resident((HD, D))],
        out_specs=[pl.BlockSpec((tm, HD), lambda i: (i, 0)),
                   pl.BlockSpec((1, HD, tm), lambda i: (i, 0, 0)),
                   pl.BlockSpec((1, 1, HD), lambda i: (i, 0, 0))],
        out_shape=[jax.ShapeDtypeStruct((T, HD), BF16),
                   jax.ShapeDtypeStruct((T // tm, HD, tm), BF16),
                   jax.ShapeDtypeStruct((T // tm, 1, HD), F32)],
        compiler_params=_params(48, ("arbitrary",)),
        name="kv_proj",
    )(x, mods, mods, w_k, w_vt)


def _moba_kernel(qt_ref, k_ref, vt_ref, km_ref, o_ref, m_scr, l_scr, acc_scr, s_scr, *, hps):
    bs = qt_ref.shape[2]
    nb = km_ref.shape[1]
    qb = pl.program_id(2)
    start = pl.multiple_of(qb * bs, bs)
    kidx = lax.broadcasted_iota(I32, (bs, bs), 0)
    qidx = lax.broadcasted_iota(I32, (bs, bs), 1)
    blk = lax.broadcasted_iota(I32, (nb, bs), 0)
    valid = blk < qb
    heads = [slice(h * HEAD_DIM, (h + 1) * HEAD_DIM) for h in range(hps)]

    biases = []
    for cs in heads:
        qt = qt_ref[0, cs, :]
        km = km_ref[0, :, cs]
        km_hi = km.astype(BF16)
        km_lo = (km - km_hi.astype(F32)).astype(BF16)
        gate = (jnp.dot(km_hi, qt, preferred_element_type=F32)
                + jnp.dot(km_lo, qt, preferred_element_type=F32))
        gate = jnp.where(valid, gate, -jnp.inf)
        beaten = jnp.zeros((nb, bs), F32)
        for jp in range(nb):
            row = gate[jp:jp + 1, :]
            wins = jnp.logical_or(row > gate, jnp.logical_and(row == gate, jp < blk))
            beaten = beaten + jnp.where(wins, 1.0, 0.0)
        biases.append(jnp.where(jnp.logical_and(valid, beaten < MOBA_TOPK), 0.0, MASK_VALUE))

    def scores(rows, slot):
        for h, cs in enumerate(heads):
            s_scr[slot, h] = jnp.dot(k_ref[rows, cs], qt_ref[0, cs, :],
                                     preferred_element_type=F32)

    own_slot = 1
    scores(pl.ds(start, bs), own_slot)
    scores(slice(0, bs), 0)

    for h, cs in enumerate(heads):
        s = jnp.where(kidx <= qidx, s_scr[own_slot, h], MASK_VALUE)
        m = jnp.max(s, axis=0, keepdims=True)
        p = jnp.exp2(s - m)
        m_scr[h] = m
        l_scr[h] = jnp.sum(p, axis=0, keepdims=True)
        acc_scr[h] = jnp.dot(vt_ref[qb, cs, :], p.astype(BF16), preferred_element_type=F32)

    for j in range(nb - 1):
        @pl.when(j < qb)
        def _(j=j):
            if j + 1 < nb - 1:
                scores(slice((j + 1) * bs, (j + 2) * bs), (j + 1) % 2)
            for h, cs in enumerate(heads):
                s = s_scr[j % 2, h] + biases[h][j:j + 1, :]
                m_old = m_scr[h]
                m_new = jnp.maximum(m_old, jnp.max(s, axis=0, keepdims=True))
                a = jnp.exp2(m_old - m_new)
                p = jnp.exp2(s - m_new)
                l_scr[h] = a * l_scr[h] + jnp.sum(p, axis=0, keepdims=True)
                acc_scr[h] = a * acc_scr[h] + jnp.dot(
                    vt_ref[j, cs, :], p.astype(BF16), preferred_element_type=F32)
                m_scr[h] = m_new

    for h, cs in enumerate(heads):
        o_ref[:, cs] = (acc_scr[h] * (1.0 / l_scr[h])).T.astype(BF16)


def _moba_call(qt, k, vt, kmeans, B, S, hps):
    T, HD = k.shape
    H = HD // HEAD_DIM
    bs = MOBA_BLOCK
    nb = S // bs
    hw = hps * HEAD_DIM
    return pl.pallas_call(
        functools.partial(_moba_kernel, hps=hps),
        grid=(B, H // hps, nb),
        in_specs=[
            pl.BlockSpec((1, hw, bs), lambda b, g, i: (b * nb + i, g, 0)),
            pl.BlockSpec((S, hw), lambda b, g, i: (b, g)),
            pl.BlockSpec((nb, hw, bs), lambda b, g, i: (b, g, 0)),
            pl.BlockSpec((1, nb, hw), lambda b, g, i: (b, 0, g)),
        ],
        out_specs=pl.BlockSpec((bs, hw), lambda b, g, i: (b * nb + i, g)),
        out_shape=jax.ShapeDtypeStruct((T, HD), BF16),
        scratch_shapes=[pltpu.VMEM((hps, 1, bs), F32), pltpu.VMEM((hps, 1, bs), F32),
                        pltpu.VMEM((hps, HEAD_DIM, bs), F32),
                        pltpu.VMEM((2, hps, bs, bs), F32)],
        compiler_params=_params(48, ("arbitrary", "arbitrary", "arbitrary")),
        name="moba_attention",
    )(qt, k, vt, kmeans.reshape(B, nb, HD))


def kernel(x, c, ada_w, ada_b, ln_g, ln_b, gm_w_in, gm_b_in, gm_lnv_g, gm_lnv_b, gm_w_s,
           gm_b_s, gm_w_out, kv_ada_w, kv_ada_b, w_kv, attn_w_q, attn_w_out, moe_w_router,
           moe_b_router, moe_w_gu, moe_b_gu, moe_w_down, moe_b_down):
    B, S, D = x.shape
    T = B * S
    depth = ada_w.shape[0]
    n_a = gm_w_in.shape[0]
    E = moe_w_router.shape[2]
    W = gm_w_in.shape[2] // 2
    C = gm_w_s.shape[2]
    H = attn_w_q.shape[2] // HEAD_DIM
    hps = next(n for n in (8, 4, 2, 1) if H % n == 0)
    assert B <= MODS_PAD_ROWS and S % MOBA_BLOCK == 0 and S % ROW_TILE == 0
    assert ROW_TILE % C == 0 and W // gm_w_s.shape[1] == LANES
    assert (D // 2) % LANES == 0 and (TOP_K_EXPERTS * T) % MOE_TILE == 0
    alpha = (2.0 * depth) ** 0.25

    c_pad = jnp.zeros((MODS_PAD_ROWS, D), F32).at[:B].set(c)
    n_mod = ada_w.shape[2] // D
    mods = _mods_call(c_pad, ada_w, ada_b).reshape(depth * MODS_PAD_ROWS * n_mod, 1, D)
    kv_mods = _mods_call(c_pad, kv_ada_w[None], kv_ada_b[None]).reshape(
        MODS_PAD_ROWS * 2, 1, D)

    def mod_row(l, which):
        return lambda b: (l * MODS_PAD_ROWS + b) * n_mod + which

    ln_g3 = ln_g.reshape(depth * 2, 1, D)
    ln_b3 = ln_b.reshape(depth * 2, 1, D)
    causal = jnp.tril(jnp.ones((C, C), F32))
    gm_w_in_b = gm_w_in.astype(BF16)
    gm_w_out_b = gm_w_out.astype(BF16)
    gm_ws_b = (gm_w_s * causal).astype(BF16)
    gm_bs_t = jnp.swapaxes(gm_b_s, 1, 2)
    attn_w_qt_b = jnp.swapaxes(attn_w_q, 1, 2).astype(BF16)
    attn_w_out_b = attn_w_out.astype(BF16)
    w_r_b = moe_w_router.astype(BF16)
    w_gu_b = moe_w_gu.astype(BF16)

    xt = x.reshape(T, D)
    k = vt = kmeans = None
    for l in range(depth):
        if l < n_a:
            y = _gmlp_in_call(
                xt, mods, mod_row(l, 0), mod_row(l, 1), gm_w_in_b, gm_b_in[:, None],
                gm_lnv_g[:, None], gm_lnv_b[:, None], gm_ws_b, gm_bs_t, l, S)
            w_o, lw = gm_w_out_b, l
        else:
            lw = l - n_a
            qt = _q_proj_call(xt, mods, mod_row(l, 0), mod_row(l, 1), attn_w_qt_b, lw, S,
                              HEAD_DIM ** -0.5 * LOG2_E)
            y = _moba_call(qt, k, vt, kmeans, B, S, hps)
            w_o = attn_w_out_b
        x1, xp, meta, counts = _proj_ln_router_call(
            y, w_o, lw, xt, mods, mod_row(l, 2), mod_row(l, 3), mod_row(l, 4),
            (ln_g3, ln_b3), 2 * l, w_r_b, moe_b_router[:, None], l, S, alpha)
        tile_expert, n_valid, tok_sorted, gate_sorted = _moe_routing(meta, counts, T, E)
        act = _moe_up_call(tile_expert, n_valid, tok_sorted, xp, w_gu_b, moe_b_gu, l)
        halves = [_moe_down_call(tile_expert, n_valid, tok_sorted, gate_sorted, act, moe_w_down,
                                 moe_b_down, l, half, T) for half in range(2)]
        xt = _moe_res_ln_call(halves[0], halves[1], x1, mods, mod_row(l, 5), (ln_g3, ln_b3),
                              2 * l + 1, S, alpha)
        if l == n_a - 1:
            hd = w_kv.shape[1] // 2
            k, vt, kmeans = _kv_proj_call(
                xt, kv_mods, lambda b: b * 2, lambda b: b * 2 + 1,
                w_kv[:, :hd].astype(BF16), w_kv[:, hd:].T.astype(BF16), S)
    return xt.reshape(B, S, D)
```

```python
import functools

import jax
import jax.numpy as jnp
from jax import lax
from jax.experimental import pallas as pl
from jax.experimental.pallas import tpu as pltpu

HEAD_DIM = 128
MOBA_BLOCK = 256
MOBA_TOPK = 3
TOP_K_EXPERTS = 4
SWIGLU_LIMIT = 7.0
SWIGLU_ALPHA = 1.702
LN_EPS = 1e-5

LANES = 128
SUBLANES = 8

ROW_TILE = 256
MOE_TILE = 512
ACC_CHUNK = 16
ROUTE_UNROLL = 8
MODS_PAD_ROWS = SUBLANES
MASK_VALUE = -1e30
LOG2_E = 1.4426950408889634
HI16 = -65536

F32 = jnp.float32
BF16 = jnp.bfloat16
I32 = jnp.int32


def _params(vmem_mb, semantics):
    return pltpu.CompilerParams(dimension_semantics=semantics,
                                vmem_limit_bytes=vmem_mb * 1024 * 1024)


def _resident(shape):
    nd = len(shape)
    return pl.BlockSpec(shape, lambda *_: (0,) * nd, pipeline_mode=pl.Buffered(1))


def _layer_resident(tail, l):
    nd = len(tail)
    return pl.BlockSpec((1,) + tuple(tail), lambda *_: (l,) + (0,) * nd,
                        pipeline_mode=pl.Buffered(1))


def _layer_norm(r, g, b):
    mu = jnp.mean(r, axis=-1, keepdims=True)
    d = r - mu
    var = jnp.mean(d * d, axis=-1, keepdims=True)
    return d * lax.rsqrt(var + LN_EPS) * g + b


def _pack_bf16_pair(lo, hi):
    lo = lax.bitcast_convert_type(lo.astype(BF16).astype(F32), I32)
    hi = lax.bitcast_convert_type(hi.astype(BF16).astype(F32), I32)
    return lax.shift_right_logical(lo, 16) | (hi & HI16)


def _unpack_bf16_pair(w):
    lo = lax.bitcast_convert_type(lax.shift_left(w, 16), F32)
    hi = lax.bitcast_convert_type(w & HI16, F32)
    return lo, hi


def _mods_kernel(c_ref, w_ref, b_ref, o_ref):
    c = c_ref[...]
    ca = (c * jax.nn.sigmoid(c)).astype(BF16)
    w = w_ref[0].astype(BF16)
    o_ref[0] = jnp.dot(ca, w, preferred_element_type=F32) + b_ref[0]


def _mods_call(c_pad, w, b, tn=1024):
    L, D, N = w.shape
    return pl.pallas_call(
        _mods_kernel,
        grid=(L, N // tn),
        in_specs=[
            pl.BlockSpec((MODS_PAD_ROWS, D), lambda l, j: (0, 0)),
            pl.BlockSpec((1, D, tn), lambda l, j: (l, 0, j)),
            pl.BlockSpec((1, 1, tn), lambda l, j: (l, 0, j)),
        ],
        out_specs=pl.BlockSpec((1, MODS_PAD_ROWS, tn), lambda l, j: (l, 0, j)),
        out_shape=jax.ShapeDtypeStruct((L, MODS_PAD_ROWS, N), F32),
        compiler_params=_params(40, ("arbitrary", "arbitrary")),
        name="adaln_mods",
    )(c_pad, w, b.reshape(L, 1, N))


def _gmlp_in_kernel(x_ref, sh_ref, sc_ref, w_ref, b_ref, lg_ref, lb_ref, ws_ref, bs_ref,
                    y_ref, u_scr, v_scr, vn_scr, *, ncol):
    tm, D = x_ref.shape
    W = y_ref.shape[1]
    _, G, C, _ = ws_ref.shape
    gd = W // G
    h = (x_ref[...] * (1.0 + sc_ref[0]) + sh_ref[0]).astype(BF16)
    for j in range(2 * W // ncol):
        z = jnp.dot(h, w_ref[0, :, j * ncol:(j + 1) * ncol], preferred_element_type=F32)
        z = jax.nn.gelu(z + b_ref[0, :, j * ncol:(j + 1) * ncol], approximate=True)
        if j * ncol < W:
            u_scr[:, j * ncol:(j + 1) * ncol] = z
        else:
            v_scr[:, j * ncol - W:(j + 1) * ncol - W] = z
    vn_scr[...] = _layer_norm(v_scr[...], lg_ref[0], lb_ref[0]).astype(BF16)
    for n in range(tm // C):
        rows = slice(n * C, (n + 1) * C)
        for g in range(G):
            cols = slice(g * gd, (g + 1) * gd)
            sv = jnp.dot(ws_ref[0, g], vn_scr[rows, cols], preferred_element_type=F32)
            sv = sv + bs_ref[0, :, g:g + 1]
            y_ref[rows, cols] = (u_scr[rows, cols] * sv).astype(BF16)


def _gmlp_in_call(x, mods, sh_row, sc_row, w_in, b_in, lnv_g, lnv_b, ws, bs_t, l, S):
    T, D = x.shape
    W = w_in.shape[2] // 2
    _, G, C, _ = ws.shape
    tm = ROW_TILE
    tpb = S // tm
    mod = lambda row: pl.BlockSpec((1, 1, D), lambda i: (row(i // tpb), 0, 0))
    return pl.pallas_call(
        functools.partial(_gmlp_in_kernel, ncol=512),
        grid=(T // tm,),
        in_specs=[
            pl.BlockSpec((tm, D), lambda i: (i, 0)),
            mod(sh_row), mod(sc_row),
            _layer_resident((D, 2 * W), l),
            _layer_resident((1, 2 * W), l),
            _layer_resident((1, W), l), _layer_resident((1, W), l),
            _layer_resident((G, C, C), l),
            _layer_resident((C, G), l),
        ],
        out_specs=pl.BlockSpec((tm, W), lambda i: (i, 0)),
        out_shape=jax.ShapeDtypeStruct((T, W), BF16),
        scratch_shapes=[pltpu.VMEM((tm, W), F32), pltpu.VMEM((tm, W), F32),
                        pltpu.VMEM((tm, W), BF16)],
        compiler_params=_params(48, ("arbitrary",)),
        name="gmlp_in",
    )(x, mods, mods, w_in, b_in, lnv_g, lnv_b, ws, bs_t)


def _proj_ln_router_kernel(a_ref, w_ref, x_ref, g1_ref, lg_ref, lb_ref, sh_ref, sc_ref,
                           wr_ref, br_ref,
                           x1_ref, xp_ref, meta_ref, cnt_ref, carry, *, alpha):
    tm, D = x_ref.shape
    E = wr_ref.shape[2]
    spt = xp_ref.shape[0] // tm
    i = pl.program_id(0)

    @pl.when(i == 0)
    def _():
        carry[...] = jnp.zeros_like(carry)

    h = jnp.dot(a_ref[...], w_ref[0], preferred_element_type=F32)
    x1 = _layer_norm(alpha * x_ref[...] + (1.0 + g1_ref[0]) * h, lg_ref[0], lb_ref[0])
    x1_ref[...] = x1
    xm = x1 * (1.0 + sc_ref[0]) + sh_ref[0]
    pk = _pack_bf16_pair(xm[:, :D // 2], xm[:, D // 2:])
    for s in range(spt):
        xp_ref[pl.ds(s, tm, stride=spt), :] = pk[:, s * LANES:(s + 1) * LANES]

    logits = jnp.dot(xm.astype(BF16), wr_ref[0], preferred_element_type=F32) + br_ref[0]
    lane = lax.broadcasted_iota(I32, (tm, E), 1).astype(F32)
    onehots, vals, idxs = [], [], []
    cur = logits
    for _k in range(TOP_K_EXPERTS):
        m = jnp.max(cur, axis=-1, keepdims=True)
        idx = jnp.min(jnp.where(cur == m, lane, float(E)), axis=-1, keepdims=True)
        oh = lane == idx
        cur = jnp.where(oh, -jnp.inf, cur)
        onehots.append(jnp.where(oh, 1.0, 0.0))
        vals.append(m)
        idxs.append(idx.astype(I32))
    exps = [jnp.exp(v - vals[0]) for v in vals]
    denom = exps[0]
    for e in exps[1:]:
        denom = denom + e
    oh_all = onehots[0]
    for oh in onehots[1:]:
        oh_all = oh_all + oh
    r_i = lax.broadcasted_iota(I32, (tm, tm), 0)
    c_i = lax.broadcasted_iota(I32, (tm, tm), 1)
    tri = jnp.where(r_i > c_i, 1.0, 0.0).astype(BF16)
    before = jnp.dot(tri, oh_all.astype(BF16), preferred_element_type=F32) + carry[...]
    mlane = lax.broadcasted_iota(I32, (tm, LANES), 1)
    meta = jnp.zeros((tm, LANES), I32)
    for k in range(TOP_K_EXPERTS):
        rank = jnp.sum(onehots[k] * before, axis=-1, keepdims=True).astype(I32)
        wk = lax.bitcast_convert_type(exps[k] / denom, I32)
        meta = jnp.where(mlane == k, idxs[k], meta)
        meta = jnp.where(mlane == TOP_K_EXPERTS + k, wk, meta)
        meta = jnp.where(mlane == 2 * TOP_K_EXPERTS + k, rank, meta)
    meta_ref[...] = meta
    carry[...] = carry[...] + jnp.sum(oh_all, axis=0, keepdims=True)
    cnt_ref[...] = carry[...].astype(I32)


def _proj_ln_router_call(a, w, lw, x, mods, g1_row, sh_row, sc_row, ln, ln_row, w_r, b_r, l,
                         S, alpha):
    T, D = x.shape
    K = a.shape[1]
    E = w_r.shape[2]
    spt = (D // 2) // LANES
    tm = ROW_TILE
    tpb = S // tm
    mod = lambda row: pl.BlockSpec((1, 1, D), lambda i: (row(i // tpb), 0, 0))
    lnspec = pl.BlockSpec((1, 1, D), lambda i: (ln_row, 0, 0))
    return pl.pallas_call(
        functools.partial(_proj_ln_router_kernel, alpha=alpha),
        grid=(T // tm,),
        in_specs=[
            pl.BlockSpec((tm, K), lambda i: (i, 0)),
            _layer_resident((K, D), lw),
            pl.BlockSpec((tm, D), lambda i: (i, 0)),
            mod(g1_row), lnspec, lnspec,
            mod(sh_row), mod(sc_row),
            _layer_resident((D, E), l), _layer_resident((1, E), l),
        ],
        out_specs=[
            pl.BlockSpec((tm, D), lambda i: (i, 0)),
            pl.BlockSpec((tm * spt, LANES), lambda i: (i, 0)),
            pl.BlockSpec((tm, LANES), lambda i: (i, 0)),
            pl.BlockSpec((1, E), lambda i: (0, 0)),
        ],
        out_shape=[
            jax.ShapeDtypeStruct((T, D), F32),
            jax.ShapeDtypeStruct((T * spt, LANES), I32),
            jax.ShapeDtypeStruct((T, LANES), I32),
            jax.ShapeDtypeStruct((1, E), I32),
        ],
        scratch_shapes=[pltpu.VMEM((1, E), F32)],
        compiler_params=_params(48, ("arbitrary",)),
        name="proj_ln_router",
    )(a, w, x, mods, ln[0], ln[1], mods, mods, w_r, b_r)


def _moe_up_kernel(te_ref, nv_ref, src_ref, xp_ref, wgu_ref, bgu_ref, act_ref,
                   g_scr, *, spt):
    tm, F = act_ref.shape
    i = pl.program_id(0)
    nt = pl.num_programs(0)
    nv = nv_ref[0]

    def gather(tile, slot):
        base = tile * tm
        last_tok = xp_ref.shape[0] // spt - 1
        for r in range(tm):
            src = pl.multiple_of(jnp.minimum(src_ref[base + r], last_tok) * spt, spt)
            g_scr[slot, r * spt:(r + 1) * spt, :] = xp_ref[pl.ds(src, spt), :]

    @pl.when(i == 0)
    def _():
        gather(0, 0)

    @pl.when(i < nv)
    def _():
        slot = i % 2
        los, his = [], []
        for s in range(spt):
            lo, hi = _unpack_bf16_pair(g_scr[slot, pl.ds(s, tm, stride=spt), :])
            los.append(lo)
            his.append(hi)
        xt = jnp.concatenate(los + his, axis=1).astype(BF16)
        gather(jnp.minimum(i + 1, nt - 1), 1 - slot)
        h = jnp.dot(xt, wgu_ref[0, 0], preferred_element_type=F32) + bgu_ref[0, 0]
        g = jnp.minimum(h[:, :F], SWIGLU_LIMIT)
        u = jnp.clip(h[:, F:], -SWIGLU_LIMIT, SWIGLU_LIMIT)
        act_ref[...] = ((u + 1.0) * g * jax.nn.sigmoid(SWIGLU_ALPHA * g)).astype(BF16)

    @pl.when(i >= nv)
    def _():
        act_ref[...] = jnp.zeros_like(act_ref)


def _moe_up_call(tile_expert, n_valid, src_row, xp, w_gu, b_gu, l):
    L, E, D, F2 = w_gu.shape
    spt = (D // 2) // LANES
    tm = MOE_TILE
    nt = tile_expert.shape[0]
    grid_spec = pltpu.PrefetchScalarGridSpec(
        num_scalar_prefetch=3,
        grid=(nt,),
        in_specs=[
            pl.BlockSpec(xp.shape, lambda i, te, nv, src: (0, 0), pipeline_mode=pl.Buffered(1)),
            pl.BlockSpec((1, 1, D, F2), lambda i, te, nv, src: (l, te[i], 0, 0)),
            pl.BlockSpec((1, 1, 1, F2), lambda i, te, nv, src: (l, te[i], 0, 0)),
        ],
        out_specs=pl.BlockSpec((tm, F2 // 2), lambda i, te, nv, src: (i, 0)),
        scratch_shapes=[
            pltpu.VMEM((2, tm * spt, LANES), I32),
        ],
    )
    return pl.pallas_call(
        functools.partial(_moe_up_kernel, spt=spt),
        grid_spec=grid_spec,
        out_shape=jax.ShapeDtypeStruct((nt * tm, F2 // 2), BF16),
        compiler_params=_params(58, ("arbitrary",)),
        name="moe_up",
    )(tile_expert, n_valid, src_row, xp, w_gu, b_gu.reshape(L, E, 1, F2))


def _moe_down_kernel(te_ref, nv_ref, dst_ref, act_ref, gate_ref, wd_ref, bd_ref, o_hbm,
                     o_scr, ylo_scr, yhi_scr, sem):
    tm = act_ref.shape[0]
    spt = ylo_scr.shape[0] // tm
    half = spt * LANES
    i = pl.program_id(0)
    last = pl.num_programs(0) - 1
    nv = nv_ref[0]

    @pl.when(i == 0)
    def _():
        o_scr[...] = jnp.zeros_like(o_scr)

    def matmul():
        y = jnp.dot(act_ref[...], wd_ref[0, 0].astype(BF16), preferred_element_type=F32)
        gate = jnp.broadcast_to(gate_ref[0], (LANES, tm)).T
        for s in range(spt):
            for scr, c0 in ((ylo_scr, s * LANES), (yhi_scr, half + s * LANES)):
                cols = slice(c0, c0 + LANES)
                scr[pl.ds(s, tm, stride=spt), :] = (y[:, cols] + bd_ref[0, 0, :, cols]) * gate

    def accumulate(tile):
        base = tile * tm
        for c0 in range(0, tm, ACC_CHUNK):
            dsts = [pl.multiple_of(dst_ref[base + r] * spt, spt)
                    for r in range(c0, c0 + ACC_CHUNK)]
            sums = []
            for k, r in enumerate(range(c0, c0 + ACC_CHUNK)):
                rows = slice(r * spt, (r + 1) * spt)
                lo, hi = _unpack_bf16_pair(o_scr[pl.ds(dsts[k], spt), :])
                sums.append(_pack_bf16_pair(lo + ylo_scr[rows, :], hi + yhi_scr[rows, :]))
            for k in range(ACC_CHUNK):
                o_scr[pl.ds(dsts[k], spt), :] = sums[k]

    @pl.when(jnp.logical_and(i == 0, i < nv))
    def _():
        matmul()

    @pl.when(jnp.logical_and(i >= 1, i < nv))
    def _():
        accumulate(i - 1)
        matmul()

    @pl.when(jnp.logical_and(i >= 1, i == nv))
    def _():
        accumulate(i - 1)

    @pl.when(i == last)
    def _():
        out = pltpu.make_async_copy(o_scr, o_hbm, sem)
        out.start()
        out.wait()


def _moe_down_call(tile_expert, n_valid, dst_row, gate_sorted, act, w_down, b_down, l,
                   n_tokens):
    L, E, F, D = w_down.shape
    spt = (D // 2) // LANES
    tm = MOE_TILE
    nt = tile_expert.shape[0]
    n_rows = (n_tokens + 1) * spt
    tile = lambda i: jnp.minimum(i, nt - 1)
    grid_spec = pltpu.PrefetchScalarGridSpec(
        num_scalar_prefetch=3,
        grid=(nt + 1,),
        in_specs=[
            pl.BlockSpec((tm, F), lambda i, te, nv, dst: (tile(i), 0)),
            pl.BlockSpec((1, 1, tm), lambda i, te, nv, dst: (tile(i), 0, 0)),
            pl.BlockSpec((1, 1, F, D), lambda i, te, nv, dst: (l, te[tile(i)], 0, 0)),
            pl.BlockSpec((1, 1, 1, D), lambda i, te, nv, dst: (l, te[tile(i)], 0, 0)),
        ],
        out_specs=pl.BlockSpec(memory_space=pl.ANY),
        scratch_shapes=[
            pltpu.VMEM((n_rows, LANES), I32),
            pltpu.VMEM((tm * spt, LANES), F32),
            pltpu.VMEM((tm * spt, LANES), F32),
            pltpu.SemaphoreType.DMA(()),
        ],
    )
    return pl.pallas_call(
        _moe_down_kernel,
        grid_spec=grid_spec,
        out_shape=jax.ShapeDtypeStruct((n_rows, LANES), I32),
        compiler_params=_params(58, ("arbitrary",)),
        name="moe_down",
    )(tile_expert, n_valid, dst_row, act, gate_sorted.reshape(nt, 1, tm), w_down,
      b_down.reshape(L, E, 1, D))


def _route_invert_kernel(dest_ref, gate_ref, tok_fill, gate_fill, tok_ref, gsort_ref, sem):
    n_pairs = dest_ref.shape[0]
    fills = [pltpu.make_async_copy(tok_fill, tok_ref, sem.at[0]),
             pltpu.make_async_copy(gate_fill, gsort_ref, sem.at[1])]
    for f in fills:
        f.start()
    for f in fills:
        f.wait()

    per_step = ROUTE_UNROLL // TOP_K_EXPERTS

    def place(b, c):
        for q in range(ROUTE_UNROLL):
            p = b * ROUTE_UNROLL + q
            d = dest_ref[p]
            tok_ref[d] = b * per_step + q // TOP_K_EXPERTS
            gsort_ref[d] = gate_ref[p]
        return c

    lax.fori_loop(0, n_pairs // ROUTE_UNROLL, place, 0)


def _route_invert_call(dest, gate, n_slots, n_tokens):
    smem = pl.BlockSpec(memory_space=pltpu.SMEM)
    hbm = pl.BlockSpec(memory_space=pl.ANY)
    return pl.pallas_call(
        _route_invert_kernel,
        in_specs=[smem, smem, hbm, hbm],
        out_specs=[smem, smem],
        out_shape=[jax.ShapeDtypeStruct((n_slots,), I32),
                   jax.ShapeDtypeStruct((n_slots,), F32)],
        scratch_shapes=[pltpu.SemaphoreType.DMA((2,))],
        name="route_invert",
    )(dest, gate, jnp.full((n_slots,), n_tokens, I32), jnp.zeros((n_slots,), F32))


def _moe_routing(meta, counts, n_tokens, n_experts):
    tm = MOE_TILE
    n_pairs = TOP_K_EXPERTS * n_tokens
    nt = n_pairs // tm + n_experts
    ei = meta[:, 0:TOP_K_EXPERTS]
    tw = lax.bitcast_convert_type(meta[:, TOP_K_EXPERTS:2 * TOP_K_EXPERTS], F32)
    rk = meta[:, 2 * TOP_K_EXPERTS:3 * TOP_K_EXPERTS]
    cnt = counts[0]
    tiles = (cnt + tm - 1) // tm
    tile_end = jnp.cumsum(tiles)
    group_start = (tile_end - tiles) * tm
    starts = jnp.sum(jnp.where(ei[..., None] == jnp.arange(n_experts, dtype=I32),
                               group_start.astype(I32), 0), axis=-1)
    dest = (starts + rk).reshape(-1)
    tok_sorted, gate_sorted = _route_invert_call(dest, tw.reshape(-1), nt * tm, n_tokens)
    n_valid = tile_end[-1:].astype(I32)
    tid = jnp.minimum(jnp.arange(nt, dtype=I32), n_valid[0] - 1)
    tile_expert = jnp.sum((tile_end[None, :] <= tid[:, None]).astype(I32), axis=1)
    return tile_expert, n_valid, tok_sorted, gate_sorted


def _moe_res_ln_kernel(o_ref_in, x1_ref, g2_ref, lg_ref, lb_ref, o_ref, *, alpha):
    tc = x1_ref.shape[0]
    spt = o_ref_in.shape[0] // tc
    los, his = [], []
    for s in range(spt):
        lo, hi = _unpack_bf16_pair(o_ref_in[pl.ds(s, tc, stride=spt), :])
        los.append(lo)
        his.append(hi)
    moe = jnp.concatenate(los + his, axis=1)
    r = alpha * x1_ref[...] + (1.0 + g2_ref[0]) * moe
    o_ref[...] = _layer_norm(r, lg_ref[0], lb_ref[0])


def _moe_res_ln_call(o_packed, x1, mods, g2_row, ln, ln_row, S, alpha):
    T, D = x1.shape
    spt = (D // 2) // LANES
    tc = ROW_TILE
    tpb = S // tc
    lnspec = pl.BlockSpec((1, 1, D), lambda i: (ln_row, 0, 0))
    return pl.pallas_call(
        functools.partial(_moe_res_ln_kernel, alpha=alpha),
        grid=(T // tc,),
        in_specs=[pl.BlockSpec((tc * spt, LANES), lambda i: (i, 0)),
                  pl.BlockSpec((tc, D), lambda i: (i, 0)),
                  pl.BlockSpec((1, 1, D), lambda i: (g2_row(i // tpb), 0, 0)),
                  lnspec, lnspec],
        out_specs=pl.BlockSpec((tc, D), lambda i: (i, 0)),
        out_shape=jax.ShapeDtypeStruct((T, D), F32),
        compiler_params=_params(40, ("arbitrary",)),
        name="moe_res_ln",
    )(o_packed, x1, mods, ln[0], ln[1])


def _q_proj_kernel(x_ref, sh_ref, sc_ref, wt_ref, o_ref, *, out_scale):
    h = (x_ref[...] * (1.0 + sc_ref[0]) + sh_ref[0]).astype(BF16)
    zt = lax.dot_general(wt_ref[0], h, (((1,), (1,)), ((), ())), preferred_element_type=F32)
    o_ref[0] = (zt * out_scale).astype(BF16)


def _q_proj_call(x, mods, sh_row, sc_row, w_t, l, S, out_scale):
    T, D = x.shape
    N = w_t.shape[1]
    tm = MOBA_BLOCK
    tpb = S // tm
    mod = lambda row: pl.BlockSpec((1, 1, D), lambda i: (row(i // tpb), 0, 0))
    return pl.pallas_call(
        functools.partial(_q_proj_kernel, out_scale=out_scale),
        grid=(T // tm,),
        in_specs=[pl.BlockSpec((tm, D), lambda i: (i, 0)), mod(sh_row), mod(sc_row),
                  _layer_resident((N, D), l)],
        out_specs=pl.BlockSpec((1, N, tm), lambda i: (i, 0, 0)),
        out_shape=jax.ShapeDtypeStruct((T // tm, N, tm), BF16),
        compiler_params=_params(48, ("arbitrary",)),
        name="q_proj",
    )(x, mods, mods, w_t)


def _kv_proj_kernel(x_ref, sh_ref, sc_ref, wk_ref, wvt_ref, k_ref, vt_ref, km_ref):
    h = (x_ref[...] * (1.0 + sc_ref[0]) + sh_ref[0]).astype(BF16)
    zk = jnp.dot(h, wk_ref[...], preferred_element_type=F32)
    k_ref[...] = zk.astype(BF16)
    km_ref[0] = jnp.mean(zk, axis=0, keepdims=True)
    zvt = lax.dot_general(wvt_ref[...], h, (((1,), (1,)), ((), ())),
                          preferred_element_type=F32)
    vt_ref[0] = zvt.astype(BF16)


def _kv_proj_call(x, mods, sh_row, sc_row, w_k, w_vt, S):
    T, D = x.shape
    HD = w_k.shape[1]
    tm = MOBA_BLOCK
    tpb = S // tm
    mod = lambda row: pl.BlockSpec((1, 1, D), lambda i: (row(i // tpb), 0, 0))
    return pl.pallas_call(
        _kv_proj_kernel,
        grid=(T // tm,),
        in_specs=[pl.BlockSpec((tm, D), lambda i: (i, 0)), mod(sh_row), mod(sc_row),
                  _resident((D, HD)), _resident((HD, D))],
        out_specs=[pl.BlockSpec((tm, HD), lambda i: (i, 0)),
                   pl.BlockSpec((1, HD, tm), lambda i: (i, 0, 0)),
                   pl.BlockSpec((1, 1, HD), lambda i: (i, 0, 0))],
        out_shape=[jax.ShapeDtypeStruct((T, HD), BF16),
                   jax.ShapeDtypeStruct((T // tm, HD, tm), BF16),
                   jax.ShapeDtypeStruct((T // tm, 1, HD), F32)],
        compiler_params=_params(48, ("arbitrary",)),
        name="kv_proj",
    )(x, mods, mods, w_k, w_vt)


def _moba_kernel(qt_ref, k_ref, vt_ref, km_ref, o_ref, m_scr, l_scr, acc_scr, s_scr, *, hps):
    bs = qt_ref.shape[2]
    nb = km_ref.shape[1]
    qb = pl.program_id(2)
    start = pl.multiple_of(qb * bs, bs)
    kidx = lax.broadcasted_iota(I32, (bs, bs), 0)
    qidx = lax.broadcasted_iota(I32, (bs, bs), 1)
    blk = lax.broadcasted_iota(I32, (nb, bs), 0)
    valid = blk < qb
    heads = [slice(h * HEAD_DIM, (h + 1) * HEAD_DIM) for h in range(hps)]

    biases = []
    for cs in heads:
        qt = qt_ref[0, cs, :]
        km = km_ref[0, :, cs]
        km_hi = km.astype(BF16)
        km_lo = (km - km_hi.astype(F32)).astype(BF16)
        gate = (jnp.dot(km_hi, qt, preferred_element_type=F32)
                + jnp.dot(km_lo, qt, preferred_element_type=F32))
        gate = jnp.where(valid, gate, -jnp.inf)
        beaten = jnp.zeros((nb, bs), F32)
        for jp in range(nb):
            row = gate[jp:jp + 1, :]
            wins = jnp.logical_or(row > gate, jnp.logical_and(row == gate, jp < blk))
            beaten = beaten + jnp.where(wins, 1.0, 0.0)
        biases.append(jnp.where(jnp.logical_and(valid, beaten < MOBA_TOPK), 0.0, MASK_VALUE))

    def scores(rows, slot):
        for h, cs in enumerate(heads):
            s_scr[slot, h] = jnp.dot(k_ref[rows, cs], qt_ref[0, cs, :],
                                     preferred_element_type=F32)

    own_slot = 1
    scores(pl.ds(start, bs), own_slot)
    scores(slice(0, bs), 0)

    for h, cs in enumerate(heads):
        s = jnp.where(kidx <= qidx, s_scr[own_slot, h], MASK_VALUE)
        m = jnp.max(s, axis=0, keepdims=True)
        p = jnp.exp2(s - m)
        m_scr[h] = m
        l_scr[h] = jnp.sum(p, axis=0, keepdims=True)
        acc_scr[h] = jnp.dot(vt_ref[qb, cs, :], p.astype(BF16), preferred_element_type=F32)

    for j in range(nb - 1):
        @pl.when(j < qb)
        def _(j=j):
            if j + 1 < nb - 1:
                scores(slice((j + 1) * bs, (j + 2) * bs), (j + 1) % 2)
            for h, cs in enumerate(heads):
                s = s_scr[j % 2, h] + biases[h][j:j + 1, :]
                m_old = m_scr[h]
                m_new = jnp.maximum(m_old, jnp.max(s, axis=0, keepdims=True))
                a = jnp.exp2(m_old - m_new)
                p = jnp.exp2(s - m_new)
                l_scr[h] = a * l_scr[h] + jnp.sum(p, axis=0, keepdims=True)
                acc_scr[h] = a * acc_scr[h] + jnp.dot(
                    vt_ref[j, cs, :], p.astype(BF16), preferred_element_type=F32)
                m_scr[h] = m_new

    for h, cs in enumerate(heads):
        o_ref[:, cs] = (acc_scr[h] * (1.0 / l_scr[h])).T.astype(BF16)


def _moba_call(qt, k, vt, kmeans, B, S, hps):
    T, HD = k.shape
    H = HD // HEAD_DIM
    bs = MOBA_BLOCK
    nb = S // bs
    hw = hps * HEAD_DIM
    return pl.pallas_call(
        functools.partial(_moba_kernel, hps=hps),
        grid=(B, H // hps, nb),
        in_specs=[
            pl.BlockSpec((1, hw, bs), lambda b, g, i: (b * nb + i, g, 0)),
            pl.BlockSpec((S, hw), lambda b, g, i: (b, g)),
            pl.BlockSpec((nb, hw, bs), lambda b, g, i: (b, g, 0)),
            pl.BlockSpec((1, nb, hw), lambda b, g, i: (b, 0, g)),
        ],
        out_specs=pl.BlockSpec((bs, hw), lambda b, g, i: (b * nb + i, g)),
        out_shape=jax.ShapeDtypeStruct((T, HD), BF16),
        scratch_shapes=[pltpu.VMEM((hps, 1, bs), F32), pltpu.VMEM((hps, 1, bs), F32),
                        pltpu.VMEM((hps, HEAD_DIM, bs), F32),
                        pltpu.VMEM((2, hps, bs, bs), F32)],
        compiler_params=_params(48, ("arbitrary", "arbitrary", "arbitrary")),
        name="moba_attention",
    )(qt, k, vt, kmeans.reshape(B, nb, HD))


def kernel(x, c, ada_w, ada_b, ln_g, ln_b, gm_w_in, gm_b_in, gm_lnv_g, gm_lnv_b, gm_w_s,
           gm_b_s, gm_w_out, kv_ada_w, kv_ada_b, w_kv, attn_w_q, attn_w_out, moe_w_router,
           moe_b_router, moe_w_gu, moe_b_gu, moe_w_down, moe_b_down):
    B, S, D = x.shape
    T = B * S
    depth = ada_w.shape[0]
    n_a = gm_w_in.shape[0]
    E = moe_w_router.shape[2]
    W = gm_w_in.shape[2] // 2
    C = gm_w_s.shape[2]
    H = attn_w_q.shape[2] // HEAD_DIM
    hps = next(n for n in (8, 4, 2, 1) if H % n == 0)
    assert B <= MODS_PAD_ROWS and S % MOBA_BLOCK == 0 and S % ROW_TILE == 0
    assert ROW_TILE % C == 0 and W // gm_w_s.shape[1] == LANES
    assert (D // 2) % LANES == 0 and (TOP_K_EXPERTS * T) % MOE_TILE == 0
    alpha = (2.0 * depth) ** 0.25

    c_pad = jnp.zeros((MODS_PAD_ROWS, D), F32).at[:B].set(c)
    n_mod = ada_w.shape[2] // D
    mods = _mods_call(c_pad, ada_w, ada_b).reshape(depth * MODS_PAD_ROWS * n_mod, 1, D)
    kv_mods = _mods_call(c_pad, kv_ada_w[None], kv_ada_b[None]).reshape(
        MODS_PAD_ROWS * 2, 1, D)

    def mod_row(l, which):
        return lambda b: (l * MODS_PAD_ROWS + b) * n_mod + which

    ln_g3 = ln_g.reshape(depth * 2, 1, D)
    ln_b3 = ln_b.reshape(depth * 2, 1, D)
    causal = jnp.tril(jnp.ones((C, C), F32))
    gm_w_in_b = gm_w_in.astype(BF16)
    gm_w_out_b = gm_w_out.astype(BF16)
    gm_ws_b = (gm_w_s * causal).astype(BF16)
    gm_bs_t = jnp.swapaxes(gm_b_s, 1, 2)
    attn_w_qt_b = jnp.swapaxes(attn_w_q, 1, 2).astype(BF16)
    attn_w_out_b = attn_w_out.astype(BF16)
    w_r_b = moe_w_router.astype(BF16)
    w_gu_b = moe_w_gu.astype(BF16)

    xt = x.reshape(T, D)
    k = vt = kmeans = None
    for l in range(depth):
        if l < n_a:
            y = _gmlp_in_call(
                xt, mods, mod_row(l, 0), mod_row(l, 1), gm_w_in_b, gm_b_in[:, None],
                gm_lnv_g[:, None], gm_lnv_b[:, None], gm_ws_b, gm_bs_t, l, S)
            w_o, lw = gm_w_out_b, l
        else:
            lw = l - n_a
            qt = _q_proj_call(xt, mods, mod_row(l, 0), mod_row(l, 1), attn_w_qt_b, lw, S,
                              HEAD_DIM ** -0.5 * LOG2_E)
            y = _moba_call(qt, k, vt, kmeans, B, S, hps)
            w_o = attn_w_out_b
        x1, xp, meta, counts = _proj_ln_router_call(
            y, w_o, lw, xt, mods, mod_row(l, 2), mod_row(l, 3), mod_row(l, 4),
            (ln_g3, ln_b3), 2 * l, w_r_b, moe_b_router[:, None], l, S, alpha)
        tile_expert, n_valid, tok_sorted, gate_sorted = _moe_routing(meta, counts, T, E)
        act = _moe_up_call(tile_expert, n_valid, tok_sorted, xp, w_gu_b, moe_b_gu, l)
        moe_out = _moe_down_call(tile_expert, n_valid, tok_sorted, gate_sorted, act,
                                 moe_w_down, moe_b_down, l, T)
        xt = _moe_res_ln_call(moe_out, x1, mods, mod_row(l, 5), (ln_g3, ln_b3), 2 * l + 1, S,
                              alpha)
        if l == n_a - 1:
            hd = w_kv.shape[1] // 2
            k, vt, kmeans = _kv_proj_call(
                xt, kv_mods, lambda b: b * 2, lambda b: b * 2 + 1,
                w_kv[:, :hd].astype(BF16), w_kv[:, hd:].T.astype(BF16), S)
    return xt.reshape(B, S, D)
```

```python
import functools

import jax
import jax.numpy as jnp
from jax import lax
from jax.experimental import pallas as pl
from jax.experimental.pallas import tpu as pltpu

HEAD_DIM = 128
MOBA_BLOCK = 256
MOBA_TOPK = 3
TOP_K_EXPERTS = 4
SWIGLU_LIMIT = 7.0
SWIGLU_ALPHA = 1.702
LN_EPS = 1e-5

LANES = 128
SUBLANES = 8

ROW_TILE = 512
MOE_TILE = 512
ACC_CHUNK = 16
ROUTE_UNROLL = 16
MODS_PAD_ROWS = SUBLANES
MASK_VALUE = -1e30
LOG2_E = 1.4426950408889634
HI16 = -65536

F32 = jnp.float32
BF16 = jnp.bfloat16
I32 = jnp.int32


def _params(vmem_mb, semantics):
    return pltpu.CompilerParams(dimension_semantics=semantics,
                                vmem_limit_bytes=vmem_mb * 1024 * 1024)


def _resident(shape):
    nd = len(shape)
    return pl.BlockSpec(shape, lambda *_: (0,) * nd, pipeline_mode=pl.Buffered(1))


def _layer_resident(tail, l):
    nd = len(tail)
    return pl.BlockSpec((1,) + tuple(tail), lambda *_: (l,) + (0,) * nd,
                        pipeline_mode=pl.Buffered(1))


def _layer_norm(r, g, b):
    mu = jnp.mean(r, axis=-1, keepdims=True)
    d = r - mu
    var = jnp.mean(d * d, axis=-1, keepdims=True)
    return d * lax.rsqrt(var + LN_EPS) * g + b


def _pack_bf16_pair(lo, hi):
    lo = lax.bitcast_convert_type(lo.astype(BF16).astype(F32), I32)
    hi = lax.bitcast_convert_type(hi.astype(BF16).astype(F32), I32)
    return lax.shift_right_logical(lo, 16) | (hi & HI16)


def _unpack_bf16_pair(w):
    lo = lax.bitcast_convert_type(lax.shift_left(w, 16), F32)
    hi = lax.bitcast_convert_type(w & HI16, F32)
    return lo, hi


def _mods_kernel(c_ref, w_ref, b_ref, o_ref):
    c = c_ref[...]
    ca = (c * jax.nn.sigmoid(c)).astype(BF16)
    w = w_ref[0].astype(BF16)
    o_ref[0] = jnp.dot(ca, w, preferred_element_type=F32) + b_ref[0]


def _mods_call(c_pad, w, b, tn=1024):
    L, D, N = w.shape
    return pl.pallas_call(
        _mods_kernel,
        grid=(L, N // tn),
        in_specs=[
            pl.BlockSpec((MODS_PAD_ROWS, D), lambda l, j: (0, 0)),
            pl.BlockSpec((1, D, tn), lambda l, j: (l, 0, j)),
            pl.BlockSpec((1, 1, tn), lambda l, j: (l, 0, j)),
        ],
        out_specs=pl.BlockSpec((1, MODS_PAD_ROWS, tn), lambda l, j: (l, 0, j)),
        out_shape=jax.ShapeDtypeStruct((L, MODS_PAD_ROWS, N), F32),
        compiler_params=_params(40, ("arbitrary", "arbitrary")),
        name="adaln_mods",
    )(c_pad, w, b.reshape(L, 1, N))


def _gmlp_in_kernel(x_ref, sh_ref, sc_ref, w_ref, b_ref, lg_ref, lb_ref, ws_ref, bs_ref,
                    y_ref, u_scr, v_scr, vn_scr, *, ncol):
    tm, D = x_ref.shape
    W = y_ref.shape[1]
    _, G, C, _ = ws_ref.shape
    gd = W // G
    h = (x_ref[...] * (1.0 + sc_ref[0]) + sh_ref[0]).astype(BF16)
    for j in range(2 * W // ncol):
        z = jnp.dot(h, w_ref[0, :, j * ncol:(j + 1) * ncol], preferred_element_type=F32)
        z = jax.nn.gelu(z + b_ref[0, :, j * ncol:(j + 1) * ncol], approximate=True)
        if j * ncol < W:
            u_scr[:, j * ncol:(j + 1) * ncol] = z
        else:
            v_scr[:, j * ncol - W:(j + 1) * ncol - W] = z
    vn_scr[...] = _layer_norm(v_scr[...], lg_ref[0], lb_ref[0]).astype(BF16)
    for n in range(tm // C):
        rows = slice(n * C, (n + 1) * C)
        for g in range(G):
            cols = slice(g * gd, (g + 1) * gd)
            sv = jnp.dot(ws_ref[0, g], vn_scr[rows, cols], preferred_element_type=F32)
            sv = sv + bs_ref[0, :, g:g + 1]
            y_ref[rows, cols] = (u_scr[rows, cols] * sv).astype(BF16)


def _gmlp_in_call(x, mods, sh_row, sc_row, w_in, b_in, lnv_g, lnv_b, ws, bs_t, l, S):
    T, D = x.shape
    W = w_in.shape[2] // 2
    _, G, C, _ = ws.shape
    tm = ROW_TILE
    tpb = S // tm
    mod = lambda row: pl.BlockSpec((1, 1, D), lambda i: (row(i // tpb), 0, 0))
    return pl.pallas_call(
        functools.partial(_gmlp_in_kernel, ncol=512),
        grid=(T // tm,),
        in_specs=[
            pl.BlockSpec((tm, D), lambda i: (i, 0)),
            mod(sh_row), mod(sc_row),
            _layer_resident((D, 2 * W), l),
            _layer_resident((1, 2 * W), l),
            _layer_resident((1, W), l), _layer_resident((1, W), l),
            _layer_resident((G, C, C), l),
            _layer_resident((C, G), l),
        ],
        out_specs=pl.BlockSpec((tm, W), lambda i: (i, 0)),
        out_shape=jax.ShapeDtypeStruct((T, W), BF16),
        scratch_shapes=[pltpu.VMEM((tm, W), F32), pltpu.VMEM((tm, W), F32),
                        pltpu.VMEM((tm, W), BF16)],
        compiler_params=_params(48, ("arbitrary",)),
        name="gmlp_in",
    )(x, mods, mods, w_in, b_in, lnv_g, lnv_b, ws, bs_t)


def _proj_ln_router_kernel(a_ref, w_ref, x_ref, g1_ref, lg_ref, lb_ref, sh_ref, sc_ref,
                           wr_ref, br_ref,
                           x1_ref, xp_ref, meta_ref, cnt_ref, carry, *, alpha):
    tm, D = x_ref.shape
    E = wr_ref.shape[2]
    spt = xp_ref.shape[0] // tm
    i = pl.program_id(0)

    @pl.when(i == 0)
    def _():
        carry[...] = jnp.zeros_like(carry)

    h = jnp.dot(a_ref[...], w_ref[0], preferred_element_type=F32)
    x1 = _layer_norm(alpha * x_ref[...] + (1.0 + g1_ref[0]) * h, lg_ref[0], lb_ref[0])
    x1_ref[...] = x1
    xm = x1 * (1.0 + sc_ref[0]) + sh_ref[0]
    pk = _pack_bf16_pair(xm[:, :D // 2], xm[:, D // 2:])
    for s in range(spt):
        xp_ref[pl.ds(s, tm, stride=spt), :] = pk[:, s * LANES:(s + 1) * LANES]

    logits = jnp.dot(xm.astype(BF16), wr_ref[0], preferred_element_type=F32) + br_ref[0]
    lane = lax.broadcasted_iota(I32, (tm, E), 1).astype(F32)
    onehots, vals, idxs = [], [], []
    cur = logits
    for _k in range(TOP_K_EXPERTS):
        m = jnp.max(cur, axis=-1, keepdims=True)
        idx = jnp.min(jnp.where(cur == m, lane, float(E)), axis=-1, keepdims=True)
        oh = lane == idx
        cur = jnp.where(oh, -jnp.inf, cur)
        onehots.append(jnp.where(oh, 1.0, 0.0))
        vals.append(m)
        idxs.append(idx.astype(I32))
    exps = [jnp.exp(v - vals[0]) for v in vals]
    denom = exps[0]
    for e in exps[1:]:
        denom = denom + e
    oh_all = onehots[0]
    for oh in onehots[1:]:
        oh_all = oh_all + oh
    r_i = lax.broadcasted_iota(I32, (tm, tm), 0)
    c_i = lax.broadcasted_iota(I32, (tm, tm), 1)
    tri = jnp.where(r_i > c_i, 1.0, 0.0).astype(BF16)
    before = jnp.dot(tri, oh_all.astype(BF16), preferred_element_type=F32) + carry[...]
    mlane = lax.broadcasted_iota(I32, (tm, LANES), 1)
    meta = jnp.zeros((tm, LANES), I32)
    for k in range(TOP_K_EXPERTS):
        rank = jnp.sum(onehots[k] * before, axis=-1, keepdims=True).astype(I32)
        wk = lax.bitcast_convert_type(exps[k] / denom, I32)
        meta = jnp.where(mlane == k, idxs[k], meta)
        meta = jnp.where(mlane == TOP_K_EXPERTS + k, wk, meta)
        meta = jnp.where(mlane == 2 * TOP_K_EXPERTS + k, rank, meta)
    meta_ref[...] = meta
    carry[...] = carry[...] + jnp.sum(oh_all, axis=0, keepdims=True)
    cnt_ref[...] = carry[...].astype(I32)


def _proj_ln_router_call(a, w, lw, x, mods, g1_row, sh_row, sc_row, ln, ln_row, w_r, b_r, l,
                         S, alpha):
    T, D = x.shape
    K = a.shape[1]
    E = w_r.shape[2]
    spt = (D // 2) // LANES
    tm = ROW_TILE
    tpb = S // tm
    mod = lambda row: pl.BlockSpec((1, 1, D), lambda i: (row(i // tpb), 0, 0))
    lnspec = pl.BlockSpec((1, 1, D), lambda i: (ln_row, 0, 0))
    return pl.pallas_call(
        functools.partial(_proj_ln_router_kernel, alpha=alpha),
        grid=(T // tm,),
        in_specs=[
            pl.BlockSpec((tm, K), lambda i: (i, 0)),
            _layer_resident((K, D), lw),
            pl.BlockSpec((tm, D), lambda i: (i, 0)),
            mod(g1_row), lnspec, lnspec,
            mod(sh_row), mod(sc_row),
            _layer_resident((D, E), l), _layer_resident((1, E), l),
        ],
        out_specs=[
            pl.BlockSpec((tm, D), lambda i: (i, 0)),
            pl.BlockSpec((tm * spt, LANES), lambda i: (i, 0)),
            pl.BlockSpec((tm, LANES), lambda i: (i, 0)),
            pl.BlockSpec((1, E), lambda i: (0, 0)),
        ],
        out_shape=[
            jax.ShapeDtypeStruct((T, D), F32),
            jax.ShapeDtypeStruct((T * spt, LANES), I32),
            jax.ShapeDtypeStruct((T, LANES), I32),
            jax.ShapeDtypeStruct((1, E), I32),
        ],
        scratch_shapes=[pltpu.VMEM((1, E), F32)],
        compiler_params=_params(48, ("arbitrary",)),
        name="proj_ln_router",
    )(a, w, x, mods, ln[0], ln[1], mods, mods, w_r, b_r)


def _moe_up_kernel(te_ref, nv_ref, src_ref, xp_ref, wgu_ref, bgu_ref, act_ref,
                   g_scr, *, spt):
    tm, F = act_ref.shape
    i = pl.program_id(0)
    nt = pl.num_programs(0)
    nv = nv_ref[0]

    def gather(tile, slot):
        base = tile * tm
        last_tok = xp_ref.shape[0] // spt - 1
        for r in range(tm):
            src = pl.multiple_of(jnp.minimum(src_ref[base + r], last_tok) * spt, spt)
            g_scr[slot, r * spt:(r + 1) * spt, :] = xp_ref[pl.ds(src, spt), :]

    @pl.when(i == 0)
    def _():
        gather(0, 0)

    @pl.when(i < nv)
    def _():
        slot = i % 2
        los, his = [], []
        for s in range(spt):
            lo, hi = _unpack_bf16_pair(g_scr[slot, pl.ds(s, tm, stride=spt), :])
            los.append(lo)
            his.append(hi)
        xt = jnp.concatenate(los + his, axis=1).astype(BF16)
        gather(jnp.minimum(i + 1, nt - 1), 1 - slot)
        h = jnp.dot(xt, wgu_ref[0, 0], preferred_element_type=F32) + bgu_ref[0, 0]
        g = jnp.minimum(h[:, :F], SWIGLU_LIMIT)
        u = jnp.clip(h[:, F:], -SWIGLU_LIMIT, SWIGLU_LIMIT)
        act_ref[...] = ((u + 1.0) * g * jax.nn.sigmoid(SWIGLU_ALPHA * g)).astype(BF16)

    @pl.when(i >= nv)
    def _():
        act_ref[...] = jnp.zeros_like(act_ref)


def _moe_up_call(tile_expert, n_valid, src_row, xp, w_gu, b_gu, l):
    L, E, D, F2 = w_gu.shape
    spt = (D // 2) // LANES
    tm = MOE_TILE
    nt = tile_expert.shape[0]
    grid_spec = pltpu.PrefetchScalarGridSpec(
        num_scalar_prefetch=3,
        grid=(nt,),
        in_specs=[
            pl.BlockSpec(xp.shape, lambda i, te, nv, src: (0, 0), pipeline_mode=pl.Buffered(1)),
            pl.BlockSpec((1, 1, D, F2), lambda i, te, nv, src: (l, te[i], 0, 0)),
            pl.BlockSpec((1, 1, 1, F2), lambda i, te, nv, src: (l, te[i], 0, 0)),
        ],
        out_specs=pl.BlockSpec((tm, F2 // 2), lambda i, te, nv, src: (i, 0)),
        scratch_shapes=[
            pltpu.VMEM((2, tm * spt, LANES), I32),
        ],
    )
    return pl.pallas_call(
        functools.partial(_moe_up_kernel, spt=spt),
        grid_spec=grid_spec,
        out_shape=jax.ShapeDtypeStruct((nt * tm, F2 // 2), BF16),
        compiler_params=_params(58, ("arbitrary",)),
        name="moe_up",
    )(tile_expert, n_valid, src_row, xp, w_gu, b_gu.reshape(L, E, 1, F2))


def _moe_down_kernel(te_ref, nv_ref, dst_ref, act_ref, gate_ref, wd_ref, bd_ref, o_hbm,
                     o_scr, ylo_scr, yhi_scr, sem):
    tm = act_ref.shape[0]
    spt = ylo_scr.shape[0] // tm
    half = spt * LANES
    i = pl.program_id(0)
    last = pl.num_programs(0) - 1
    nv = nv_ref[0]

    @pl.when(i == 0)
    def _():
        o_scr[...] = jnp.zeros_like(o_scr)

    def matmul():
        y = jnp.dot(act_ref[...], wd_ref[0, 0].astype(BF16), preferred_element_type=F32)
        gate = jnp.broadcast_to(gate_ref[0], (LANES, tm)).T
        for s in range(spt):
            for scr, c0 in ((ylo_scr, s * LANES), (yhi_scr, half + s * LANES)):
                cols = slice(c0, c0 + LANES)
                scr[pl.ds(s, tm, stride=spt), :] = (y[:, cols] + bd_ref[0, 0, :, cols]) * gate

    def accumulate(tile):
        base = tile * tm
        for c0 in range(0, tm, ACC_CHUNK):
            dsts = [pl.multiple_of(dst_ref[base + r] * spt, spt)
                    for r in range(c0, c0 + ACC_CHUNK)]
            sums = []
            for k, r in enumerate(range(c0, c0 + ACC_CHUNK)):
                rows = slice(r * spt, (r + 1) * spt)
                lo, hi = _unpack_bf16_pair(o_scr[pl.ds(dsts[k], spt), :])
                sums.append(_pack_bf16_pair(lo + ylo_scr[rows, :], hi + yhi_scr[rows, :]))
            for k in range(ACC_CHUNK):
                o_scr[pl.ds(dsts[k], spt), :] = sums[k]

    @pl.when(jnp.logical_and(i == 0, i < nv))
    def _():
        matmul()

    @pl.when(jnp.logical_and(i >= 1, i < nv))
    def _():
        accumulate(i - 1)
        matmul()

    @pl.when(jnp.logical_and(i >= 1, i == nv))
    def _():
        accumulate(i - 1)

    @pl.when(i == last)
    def _():
        out = pltpu.make_async_copy(o_scr, o_hbm, sem)
        out.start()
        out.wait()


def _moe_down_call(tile_expert, n_valid, dst_row, gate_sorted, act, w_down, b_down, l,
                   n_tokens):
    L, E, F, D = w_down.shape
    spt = (D // 2) // LANES
    tm = MOE_TILE
    nt = tile_expert.shape[0]
    n_rows = (n_tokens + 1) * spt
    tile = lambda i: jnp.minimum(i, nt - 1)
    grid_spec = pltpu.PrefetchScalarGridSpec(
        num_scalar_prefetch=3,
        grid=(nt + 1,),
        in_specs=[
            pl.BlockSpec((tm, F), lambda i, te, nv, dst: (tile(i), 0)),
            pl.BlockSpec((1, 1, tm), lambda i, te, nv, dst: (tile(i), 0, 0)),
            pl.BlockSpec((1, 1, F, D), lambda i, te, nv, dst: (l, te[tile(i)], 0, 0)),
            pl.BlockSpec((1, 1, 1, D), lambda i, te, nv, dst: (l, te[tile(i)], 0, 0)),
        ],
        out_specs=pl.BlockSpec(memory_space=pl.ANY),
        scratch_shapes=[
            pltpu.VMEM((n_rows, LANES), I32),
            pltpu.VMEM((tm * spt, LANES), F32),
            pltpu.VMEM((tm * spt, LANES), F32),
            pltpu.SemaphoreType.DMA(()),
        ],
    )
    return pl.pallas_call(
        _moe_down_kernel,
        grid_spec=grid_spec,
        out_shape=jax.ShapeDtypeStruct((n_rows, LANES), I32),
        compiler_params=_params(58, ("arbitrary",)),
        name="moe_down",
    )(tile_expert, n_valid, dst_row, act, gate_sorted.reshape(nt, 1, tm), w_down,
      b_down.reshape(L, E, 1, D))


def _route_invert_kernel(dest_ref, gate_ref, tok_fill, gate_fill, tok_ref, gsort_ref, sem):
    n_pairs = dest_ref.shape[0]
    fills = [pltpu.make_async_copy(tok_fill, tok_ref, sem.at[0]),
             pltpu.make_async_copy(gate_fill, gsort_ref, sem.at[1])]
    for f in fills:
        f.start()
    for f in fills:
        f.wait()

    per_step = ROUTE_UNROLL // TOP_K_EXPERTS

    def place(b, c):
        for q in range(ROUTE_UNROLL):
            p = b * ROUTE_UNROLL + q
            d = dest_ref[p]
            tok_ref[d] = b * per_step + q // TOP_K_EXPERTS
            gsort_ref[d] = gate_ref[p]
        return c

    lax.fori_loop(0, n_pairs // ROUTE_UNROLL, place, 0)


def _route_invert_call(dest, gate, n_slots, n_tokens):
    smem = pl.BlockSpec(memory_space=pltpu.SMEM)
    hbm = pl.BlockSpec(memory_space=pl.ANY)
    return pl.pallas_call(
        _route_invert_kernel,
        in_specs=[smem, smem, hbm, hbm],
        out_specs=[smem, smem],
        out_shape=[jax.ShapeDtypeStruct((n_slots,), I32),
                   jax.ShapeDtypeStruct((n_slots,), F32)],
        scratch_shapes=[pltpu.SemaphoreType.DMA((2,))],
        name="route_invert",
    )(dest, gate, jnp.full((n_slots,), n_tokens, I32), jnp.zeros((n_slots,), F32))


def _moe_routing(meta, counts, n_tokens, n_experts):
    tm = MOE_TILE
    n_pairs = TOP_K_EXPERTS * n_tokens
    nt = n_pairs // tm + n_experts
    ei = meta[:, 0:TOP_K_EXPERTS]
    tw = lax.bitcast_convert_type(meta[:, TOP_K_EXPERTS:2 * TOP_K_EXPERTS], F32)
    rk = meta[:, 2 * TOP_K_EXPERTS:3 * TOP_K_EXPERTS]
    cnt = counts[0]
    tiles = (cnt + tm - 1) // tm
    tile_end = jnp.cumsum(tiles)
    group_start = (tile_end - tiles) * tm
    starts = jnp.sum(jnp.where(ei[..., None] == jnp.arange(n_experts, dtype=I32),
                               group_start.astype(I32), 0), axis=-1)
    dest = (starts + rk).reshape(-1)
    tok_sorted, gate_sorted = _route_invert_call(dest, tw.reshape(-1), nt * tm, n_tokens)
    n_valid = tile_end[-1:].astype(I32)
    tid = jnp.minimum(jnp.arange(nt, dtype=I32), n_valid[0] - 1)
    tile_expert = jnp.sum((tile_end[None, :] <= tid[:, None]).astype(I32), axis=1)
    return tile_expert, n_valid, tok_sorted, gate_sorted


def _moe_res_ln_kernel(o_ref_in, x1_ref, g2_ref, lg_ref, lb_ref, o_ref, *, alpha):
    tc = x1_ref.shape[0]
    spt = o_ref_in.shape[0] // tc
    los, his = [], []
    for s in range(spt):
        lo, hi = _unpack_bf16_pair(o_ref_in[pl.ds(s, tc, stride=spt), :])
        los.append(lo)
        his.append(hi)
    moe = jnp.concatenate(los + his, axis=1)
    r = alpha * x1_ref[...] + (1.0 + g2_ref[0]) * moe
    o_ref[...] = _layer_norm(r, lg_ref[0], lb_ref[0])


def _moe_res_ln_call(o_packed, x1, mods, g2_row, ln, ln_row, S, alpha):
    T, D = x1.shape
    spt = (D // 2) // LANES
    tc = ROW_TILE
    tpb = S // tc
    lnspec = pl.BlockSpec((1, 1, D), lambda i: (ln_row, 0, 0))
    return pl.pallas_call(
        functools.partial(_moe_res_ln_kernel, alpha=alpha),
        grid=(T // tc,),
        in_specs=[pl.BlockSpec((tc * spt, LANES), lambda i: (i, 0)),
                  pl.BlockSpec((tc, D), lambda i: (i, 0)),
                  pl.BlockSpec((1, 1, D), lambda i: (g2_row(i // tpb), 0, 0)),
                  lnspec, lnspec],
        out_specs=pl.BlockSpec((tc, D), lambda i: (i, 0)),
        out_shape=jax.ShapeDtypeStruct((T, D), F32),
        compiler_params=_params(40, ("arbitrary",)),
        name="moe_res_ln",
    )(o_packed, x1, mods, ln[0], ln[1])


def _q_proj_kernel(x_ref, sh_ref, sc_ref, wt_ref, o_ref, *, out_scale):
    h = (x_ref[...] * (1.0 + sc_ref[0]) + sh_ref[0]).astype(BF16)
    zt = lax.dot_general(wt_ref[0], h, (((1,), (1,)), ((), ())), preferred_element_type=F32)
    o_ref[0] = (zt * out_scale).astype(BF16)


def _q_proj_call(x, mods, sh_row, sc_row, w_t, l, S, out_scale):
    T, D = x.shape
    N = w_t.shape[1]
    tm = MOBA_BLOCK
    tpb = S // tm
    mod = lambda row: pl.BlockSpec((1, 1, D), lambda i: (row(i // tpb), 0, 0))
    return pl.pallas_call(
        functools.partial(_q_proj_kernel, out_scale=out_scale),
        grid=(T // tm,),
        in_specs=[pl.BlockSpec((tm, D), lambda i: (i, 0)), mod(sh_row), mod(sc_row),
                  _layer_resident((N, D), l)],
        out_specs=pl.BlockSpec((1, N, tm), lambda i: (i, 0, 0)),
        out_shape=jax.ShapeDtypeStruct((T // tm, N, tm), BF16),
        compiler_params=_params(48, ("arbitrary",)),
        name="q_proj",
    )(x, mods, mods, w_t)


def _kv_proj_kernel(x_ref, sh_ref, sc_ref, wk_ref, wvt_ref, k_ref, vt_ref, km_ref):
    h = (x_ref[...] * (1.0 + sc_ref[0]) + sh_ref[0]).astype(BF16)
    zk = jnp.dot(h, wk_ref[...], preferred_element_type=F32)
    k_ref[...] = zk.astype(BF16)
    km_ref[0] = jnp.mean(zk, axis=0, keepdims=True)
    zvt = lax.dot_general(wvt_ref[...], h, (((1,), (1,)), ((), ())),
                          preferred_element_type=F32)
    vt_ref[0] = zvt.astype(BF16)


def _kv_proj_call(x, mods, sh_row, sc_row, w_k, w_vt, S):
    T, D = x.shape
    HD = w_k.shape[1]
    tm = MOBA_BLOCK
    tpb = S // tm
    mod = lambda row: pl.BlockSpec((1, 1, D), lambda i: (row(i // tpb), 0, 0))
    return pl.pallas_call(
        _kv_proj_kernel,
        grid=(T // tm,),
        in_specs=[pl.BlockSpec((tm, D), lambda i: (i, 0)), mod(sh_row), mod(sc_row),
                  _resident((D, HD)), _resident((HD, D))],
        out_specs=[pl.BlockSpec((tm, HD), lambda i: (i, 0)),
                   pl.BlockSpec((1, HD, tm), lambda i: (i, 0, 0)),
                   pl.BlockSpec((1, 1, HD), lambda i: (i, 0, 0))],
        out_shape=[jax.ShapeDtypeStruct((T, HD), BF16),
                   jax.ShapeDtypeStruct((T // tm, HD, tm), BF16),
                   jax.ShapeDtypeStruct((T // tm, 1, HD), F32)],
        compiler_params=_params(48, ("arbitrary",)),
        name="kv_proj",
    )(x, mods, mods, w_k, w_vt)


def _moba_kernel(qt_ref, k_ref, vt_ref, km_ref, o_ref, m_scr, l_scr, acc_scr, s_scr, *, hps):
    bs = qt_ref.shape[2]
    nb = km_ref.shape[1]
    qb = pl.program_id(2)
    start = pl.multiple_of(qb * bs, bs)
    kidx = lax.broadcasted_iota(I32, (bs, bs), 0)
    qidx = lax.broadcasted_iota(I32, (bs, bs), 1)
    blk = lax.broadcasted_iota(I32, (nb, bs), 0)
    valid = blk < qb
    heads = [slice(h * HEAD_DIM, (h + 1) * HEAD_DIM) for h in range(hps)]

    biases = []
    for cs in heads:
        qt = qt_ref[0, cs, :]
        km = km_ref[0, :, cs]
        km_hi = km.astype(BF16)
        km_lo = (km - km_hi.astype(F32)).astype(BF16)
        gate = (jnp.dot(km_hi, qt, preferred_element_type=F32)
                + jnp.dot(km_lo, qt, preferred_element_type=F32))
        gate = jnp.where(valid, gate, -jnp.inf)
        beaten = jnp.zeros((nb, bs), F32)
        for jp in range(nb):
            row = gate[jp:jp + 1, :]
            wins = jnp.logical_or(row > gate, jnp.logical_and(row == gate, jp < blk))
            beaten = beaten + jnp.where(wins, 1.0, 0.0)
        biases.append(jnp.where(jnp.logical_and(valid, beaten < MOBA_TOPK), 0.0, MASK_VALUE))

    def scores(rows, slot):
        for h, cs in enumerate(heads):
            s_scr[slot, h] = jnp.dot(k_ref[rows, cs], qt_ref[0, cs, :],
                                     preferred_element_type=F32)

    own_slot = 1
    scores(pl.ds(start, bs), own_slot)
    scores(slice(0, bs), 0)

    for h, cs in enumerate(heads):
        s = jnp.where(kidx <= qidx, s_scr[own_slot, h], MASK_VALUE)
        m = jnp.max(s, axis=0, keepdims=True)
        p = jnp.exp2(s - m)
        m_scr[h] = m
        l_scr[h] = jnp.sum(p, axis=0, keepdims=True)
        acc_scr[h] = jnp.dot(vt_ref[qb, cs, :], p.astype(BF16), preferred_element_type=F32)

    for j in range(nb - 1):
        @pl.when(j < qb)
        def _(j=j):
            if j + 1 < nb - 1:
                scores(slice((j + 1) * bs, (j + 2) * bs), (j + 1) % 2)
            for h, cs in enumerate(heads):
                bias = biases[h][j:j + 1, :]
                m_old = m_scr[h]
                m_new = jnp.maximum(
                    m_old, jnp.max(s_scr[j % 2, h], axis=0, keepdims=True) + bias)
                a = jnp.exp2(m_old - m_new)
                shift = m_new - bias
                p = jnp.exp2(s_scr[j % 2, h] - shift)
                l_scr[h] = a * l_scr[h] + jnp.sum(p, axis=0, keepdims=True)
                acc_scr[h] = a * acc_scr[h] + jnp.dot(
                    vt_ref[j, cs, :], p.astype(BF16), preferred_element_type=F32)
                m_scr[h] = m_new

    for h, cs in enumerate(heads):
        o_ref[:, cs] = (acc_scr[h] * (1.0 / l_scr[h])).T.astype(BF16)


def _moba_call(qt, k, vt, kmeans, B, S, hps):
    T, HD = k.shape
    H = HD // HEAD_DIM
    bs = MOBA_BLOCK
    nb = S // bs
    hw = hps * HEAD_DIM
    return pl.pallas_call(
        functools.partial(_moba_kernel, hps=hps),
        grid=(B, H // hps, nb),
        in_specs=[
            pl.BlockSpec((1, hw, bs), lambda b, g, i: (b * nb + i, g, 0)),
            pl.BlockSpec((S, hw), lambda b, g, i: (b, g)),
            pl.BlockSpec((nb, hw, bs), lambda b, g, i: (b, g, 0)),
            pl.BlockSpec((1, nb, hw), lambda b, g, i: (b, 0, g)),
        ],
        out_specs=pl.BlockSpec((bs, hw), lambda b, g, i: (b * nb + i, g)),
        out_shape=jax.ShapeDtypeStruct((T, HD), BF16),
        scratch_shapes=[pltpu.VMEM((hps, 1, bs), F32), pltpu.VMEM((hps, 1, bs), F32),
                        pltpu.VMEM((hps, HEAD_DIM, bs), F32),
                        pltpu.VMEM((2, hps, bs, bs), F32)],
        compiler_params=_params(48, ("arbitrary", "arbitrary", "arbitrary")),
        name="moba_attention",
    )(qt, k, vt, kmeans.reshape(B, nb, HD))


def kernel(x, c, ada_w, ada_b, ln_g, ln_b, gm_w_in, gm_b_in, gm_lnv_g, gm_lnv_b, gm_w_s,
           gm_b_s, gm_w_out, kv_ada_w, kv_ada_b, w_kv, attn_w_q, attn_w_out, moe_w_router,
           moe_b_router, moe_w_gu, moe_b_gu, moe_w_down, moe_b_down):
    B, S, D = x.shape
    T = B * S
    depth = ada_w.shape[0]
    n_a = gm_w_in.shape[0]
    E = moe_w_router.shape[2]
    W = gm_w_in.shape[2] // 2
    C = gm_w_s.shape[2]
    H = attn_w_q.shape[2] // HEAD_DIM
    hps = next(n for n in (8, 4, 2, 1) if H % n == 0)
    assert B <= MODS_PAD_ROWS and S % MOBA_BLOCK == 0 and S % ROW_TILE == 0
    assert ROW_TILE % C == 0 and W // gm_w_s.shape[1] == LANES
    assert (D // 2) % LANES == 0 and (TOP_K_EXPERTS * T) % MOE_TILE == 0
    alpha = (2.0 * depth) ** 0.25

    c_pad = jnp.zeros((MODS_PAD_ROWS, D), F32).at[:B].set(c)
    n_mod = ada_w.shape[2] // D
    mods = _mods_call(c_pad, ada_w, ada_b).reshape(depth * MODS_PAD_ROWS * n_mod, 1, D)
    kv_mods = _mods_call(c_pad, kv_ada_w[None], kv_ada_b[None]).reshape(
        MODS_PAD_ROWS * 2, 1, D)

    def mod_row(l, which):
        return lambda b: (l * MODS_PAD_ROWS + b) * n_mod + which

    ln_g3 = ln_g.reshape(depth * 2, 1, D)
    ln_b3 = ln_b.reshape(depth * 2, 1, D)
    causal = jnp.tril(jnp.ones((C, C), F32))
    gm_w_in_b = gm_w_in.astype(BF16)
    gm_w_out_b = gm_w_out.astype(BF16)
    gm_ws_b = (gm_w_s * causal).astype(BF16)
    gm_bs_t = jnp.swapaxes(gm_b_s, 1, 2)
    attn_w_qt_b = jnp.swapaxes(attn_w_q, 1, 2).astype(BF16)
    attn_w_out_b = attn_w_out.astype(BF16)
    w_r_b = moe_w_router.astype(BF16)
    w_gu_b = moe_w_gu.astype(BF16)

    xt = x.reshape(T, D)
    k = vt = kmeans = None
    for l in range(depth):
        if l < n_a:
            y = _gmlp_in_call(
                xt, mods, mod_row(l, 0), mod_row(l, 1), gm_w_in_b, gm_b_in[:, None],
                gm_lnv_g[:, None], gm_lnv_b[:, None], gm_ws_b, gm_bs_t, l, S)
            w_o, lw = gm_w_out_b, l
        else:
            lw = l - n_a
            qt = _q_proj_call(xt, mods, mod_row(l, 0), mod_row(l, 1), attn_w_qt_b, lw, S,
                              HEAD_DIM ** -0.5 * LOG2_E)
            y = _moba_call(qt, k, vt, kmeans, B, S, hps)
            w_o = attn_w_out_b
        x1, xp, meta, counts = _proj_ln_router_call(
            y, w_o, lw, xt, mods, mod_row(l, 2), mod_row(l, 3), mod_row(l, 4),
            (ln_g3, ln_b3), 2 * l, w_r_b, moe_b_router[:, None], l, S, alpha)
        tile_expert, n_valid, tok_sorted, gate_sorted = _moe_routing(meta, counts, T, E)
        act = _moe_up_call(tile_expert, n_valid, tok_sorted, xp, w_gu_b, moe_b_gu, l)
        moe_out = _moe_down_call(tile_expert, n_valid, tok_sorted, gate_sorted, act,
                                 moe_w_down, moe_b_down, l, T)
        xt = _moe_res_ln_call(moe_out, x1, mods, mod_row(l, 5), (ln_g3, ln_b3), 2 * l + 1, S,
                              alpha)
        if l == n_a - 1:
            hd = w_kv.shape[1] // 2
            k, vt, kmeans = _kv_proj_call(
                xt, kv_mods, lambda b: b * 2, lambda b: b * 2 + 1,
                w_kv[:, :hd].astype(BF16), w_kv[:, hd:].T.astype(BF16), S)
    return xt.reshape(B, S, D)
```

```python
import functools

import jax
import jax.numpy as jnp
from jax import lax
from jax.experimental import pallas as pl
from jax.experimental.pallas import tpu as pltpu

HEAD_DIM = 128
MOBA_BLOCK = 256
MOBA_TOPK = 3
TOP_K_EXPERTS = 4
SWIGLU_LIMIT = 7.0
SWIGLU_ALPHA = 1.702
LN_EPS = 1e-5

LANES = 128
SUBLANES = 8

ROW_TILE = 512
MOE_TILE = 512
ACC_CHUNK = 16
ROUTE_UNROLL = 16
META_ROWS = 16
MODS_PAD_ROWS = SUBLANES
MASK_VALUE = -1e30
LOG2_E = 1.4426950408889634
HI16 = -65536

F32 = jnp.float32
BF16 = jnp.bfloat16
I32 = jnp.int32


def _params(vmem_mb, semantics):
    return pltpu.CompilerParams(dimension_semantics=semantics,
                                vmem_limit_bytes=vmem_mb * 1024 * 1024)


def _resident(shape):
    nd = len(shape)
    return pl.BlockSpec(shape, lambda *_: (0,) * nd, pipeline_mode=pl.Buffered(1))


def _layer_resident(tail, l):
    nd = len(tail)
    return pl.BlockSpec((1,) + tuple(tail), lambda *_: (l,) + (0,) * nd,
                        pipeline_mode=pl.Buffered(1))


def _layer_norm(r, g, b):
    mu = jnp.mean(r, axis=-1, keepdims=True)
    d = r - mu
    var = jnp.mean(d * d, axis=-1, keepdims=True)
    return d * lax.rsqrt(var + LN_EPS) * g + b


def _pack_bf16_pair(lo, hi):
    lo = lax.bitcast_convert_type(lo.astype(BF16).astype(F32), I32)
    hi = lax.bitcast_convert_type(hi.astype(BF16).astype(F32), I32)
    return lax.shift_right_logical(lo, 16) | (hi & HI16)


def _unpack_bf16_pair(w):
    lo = lax.bitcast_convert_type(lax.shift_left(w, 16), F32)
    hi = lax.bitcast_convert_type(w & HI16, F32)
    return lo, hi


def _mods_kernel(c_ref, w_ref, b_ref, o_ref):
    c = c_ref[...]
    ca = (c * jax.nn.sigmoid(c)).astype(BF16)
    w = w_ref[0].astype(BF16)
    o_ref[0] = jnp.dot(ca, w, preferred_element_type=F32) + b_ref[0]


def _mods_call(c_pad, w, b, tn=1024):
    L, D, N = w.shape
    return pl.pallas_call(
        _mods_kernel,
        grid=(L, N // tn),
        in_specs=[
            pl.BlockSpec((MODS_PAD_ROWS, D), lambda l, j: (0, 0)),
            pl.BlockSpec((1, D, tn), lambda l, j: (l, 0, j)),
            pl.BlockSpec((1, 1, tn), lambda l, j: (l, 0, j)),
        ],
        out_specs=pl.BlockSpec((1, MODS_PAD_ROWS, tn), lambda l, j: (l, 0, j)),
        out_shape=jax.ShapeDtypeStruct((L, MODS_PAD_ROWS, N), F32),
        compiler_params=_params(40, ("arbitrary", "arbitrary")),
        name="adaln_mods",
    )(c_pad, w, b.reshape(L, 1, N))


def _gmlp_in_kernel(x_ref, sh_ref, sc_ref, w_ref, b_ref, lg_ref, lb_ref, ws_ref, bs_ref,
                    y_ref, u_scr, v_scr, vn_scr, *, ncol):
    tm, D = x_ref.shape
    W = y_ref.shape[1]
    _, G, C, _ = ws_ref.shape
    gd = W // G
    h = (x_ref[...] * (1.0 + sc_ref[0]) + sh_ref[0]).astype(BF16)
    for j in range(2 * W // ncol):
        z = jnp.dot(h, w_ref[0, :, j * ncol:(j + 1) * ncol], preferred_element_type=F32)
        z = jax.nn.gelu(z + b_ref[0, :, j * ncol:(j + 1) * ncol], approximate=True)
        if j * ncol < W:
            u_scr[:, j * ncol:(j + 1) * ncol] = z
        else:
            v_scr[:, j * ncol - W:(j + 1) * ncol - W] = z
    vn_scr[...] = _layer_norm(v_scr[...], lg_ref[0], lb_ref[0]).astype(BF16)
    for n in range(tm // C):
        rows = slice(n * C, (n + 1) * C)
        for g in range(G):
            cols = slice(g * gd, (g + 1) * gd)
            sv = jnp.dot(ws_ref[0, g], vn_scr[rows, cols], preferred_element_type=F32)
            sv = sv + bs_ref[0, :, g:g + 1]
            y_ref[rows, cols] = (u_scr[rows, cols] * sv).astype(BF16)


def _gmlp_in_call(x, mods, sh_row, sc_row, w_in, b_in, lnv_g, lnv_b, ws, bs_t, l, S):
    T, D = x.shape
    W = w_in.shape[2] // 2
    _, G, C, _ = ws.shape
    tm = ROW_TILE
    tpb = S // tm
    mod = lambda row: pl.BlockSpec((1, 1, D), lambda i: (row(i // tpb), 0, 0))
    return pl.pallas_call(
        functools.partial(_gmlp_in_kernel, ncol=512),
        grid=(T // tm,),
        in_specs=[
            pl.BlockSpec((tm, D), lambda i: (i, 0)),
            mod(sh_row), mod(sc_row),
            _layer_resident((D, 2 * W), l),
            _layer_resident((1, 2 * W), l),
            _layer_resident((1, W), l), _layer_resident((1, W), l),
            _layer_resident((G, C, C), l),
            _layer_resident((C, G), l),
        ],
        out_specs=pl.BlockSpec((tm, W), lambda i: (i, 0)),
        out_shape=jax.ShapeDtypeStruct((T, W), BF16),
        scratch_shapes=[pltpu.VMEM((tm, W), F32), pltpu.VMEM((tm, W), F32),
                        pltpu.VMEM((tm, W), BF16)],
        compiler_params=_params(48, ("arbitrary",)),
        name="gmlp_in",
    )(x, mods, mods, w_in, b_in, lnv_g, lnv_b, ws, bs_t)


def _proj_ln_router_kernel(a_ref, w_ref, x_ref, g1_ref, lg_ref, lb_ref, sh_ref, sc_ref,
                           wr_ref, br_ref,
                           x1_ref, xp_ref, meta_ref, cnt_ref, carry, *, alpha):
    tm, D = x_ref.shape
    E = wr_ref.shape[1]
    spt = xp_ref.shape[0] // tm
    i = pl.program_id(0)

    @pl.when(i == 0)
    def _():
        carry[...] = jnp.zeros_like(carry)

    h = jnp.dot(a_ref[...], w_ref[0], preferred_element_type=F32)
    x1 = _layer_norm(alpha * x_ref[...] + (1.0 + g1_ref[0]) * h, lg_ref[0], lb_ref[0])
    x1_ref[...] = x1
    xm = x1 * (1.0 + sc_ref[0]) + sh_ref[0]
    pk = _pack_bf16_pair(xm[:, :D // 2], xm[:, D // 2:])
    for s in range(spt):
        xp_ref[pl.ds(s, tm, stride=spt), :] = pk[:, s * LANES:(s + 1) * LANES]

    logits = lax.dot_general(wr_ref[0], xm.astype(BF16), (((1,), (1,)), ((), ())),
                             preferred_element_type=F32) + br_ref[0]
    sub = lax.broadcasted_iota(I32, (E, tm), 0).astype(F32)
    onehots, vals, idxs = [], [], []
    cur = logits
    for _k in range(TOP_K_EXPERTS):
        m = jnp.max(cur, axis=0, keepdims=True)
        idx = jnp.min(jnp.where(cur == m, sub, float(E)), axis=0, keepdims=True)
        oh = sub == idx
        cur = jnp.where(oh, -jnp.inf, cur)
        onehots.append(jnp.where(oh, 1.0, 0.0))
        vals.append(m)
        idxs.append(idx.astype(I32))
    exps = [jnp.exp(v - vals[0]) for v in vals]
    denom = exps[0]
    for e in exps[1:]:
        denom = denom + e
    oh_all = onehots[0]
    for oh in onehots[1:]:
        oh_all = oh_all + oh
    r_i = lax.broadcasted_iota(I32, (tm, tm), 0)
    c_i = lax.broadcasted_iota(I32, (tm, tm), 1)
    earlier = jnp.where(r_i < c_i, 1.0, 0.0).astype(BF16)
    before = jnp.dot(oh_all.astype(BF16), earlier, preferred_element_type=F32) + carry[...]
    row = lax.broadcasted_iota(I32, (META_ROWS, tm), 0)
    meta = jnp.zeros((META_ROWS, tm), I32)
    for k in range(TOP_K_EXPERTS):
        rank = jnp.sum(onehots[k] * before, axis=0, keepdims=True).astype(I32)
        wk = lax.bitcast_convert_type(exps[k] / denom, I32)
        meta = jnp.where(row == k, idxs[k], meta)
        meta = jnp.where(row == TOP_K_EXPERTS + k, wk, meta)
        meta = jnp.where(row == 2 * TOP_K_EXPERTS + k, rank, meta)
    meta_ref[...] = meta
    carry[...] = carry[...] + jnp.sum(oh_all, axis=1, keepdims=True)
    cnt_ref[...] = carry[...].astype(I32)


def _proj_ln_router_call(a, w, lw, x, mods, g1_row, sh_row, sc_row, ln, ln_row, w_r, b_r, l,
                         S, alpha):
    T, D = x.shape
    K = a.shape[1]
    E = w_r.shape[1]
    spt = (D // 2) // LANES
    tm = ROW_TILE
    tpb = S // tm
    mod = lambda row: pl.BlockSpec((1, 1, D), lambda i: (row(i // tpb), 0, 0))
    lnspec = pl.BlockSpec((1, 1, D), lambda i: (ln_row, 0, 0))
    return pl.pallas_call(
        functools.partial(_proj_ln_router_kernel, alpha=alpha),
        grid=(T // tm,),
        in_specs=[
            pl.BlockSpec((tm, K), lambda i: (i, 0)),
            _layer_resident((K, D), lw),
            pl.BlockSpec((tm, D), lambda i: (i, 0)),
            mod(g1_row), lnspec, lnspec,
            mod(sh_row), mod(sc_row),
            _layer_resident((E, D), l), _layer_resident((E, 1), l),
        ],
        out_specs=[
            pl.BlockSpec((tm, D), lambda i: (i, 0)),
            pl.BlockSpec((tm * spt, LANES), lambda i: (i, 0)),
            pl.BlockSpec((META_ROWS, tm), lambda i: (0, i)),
            pl.BlockSpec((E, 1), lambda i: (0, 0)),
        ],
        out_shape=[
            jax.ShapeDtypeStruct((T, D), F32),
            jax.ShapeDtypeStruct((T * spt, LANES), I32),
            jax.ShapeDtypeStruct((META_ROWS, T), I32),
            jax.ShapeDtypeStruct((E, 1), I32),
        ],
        scratch_shapes=[pltpu.VMEM((E, 1), F32)],
        compiler_params=_params(48, ("arbitrary",)),
        name="proj_ln_router",
    )(a, w, x, mods, ln[0], ln[1], mods, mods, w_r, b_r)


def _moe_up_kernel(te_ref, nv_ref, src_ref, xp_ref, wgu_ref, bgu_ref, act_ref,
                   g_scr, *, spt):
    tm, F = act_ref.shape
    i = pl.program_id(0)
    nt = pl.num_programs(0)
    nv = nv_ref[0]

    def gather(tile, slot):
        base = tile * tm
        last_tok = xp_ref.shape[0] // spt - 1
        for r in range(tm):
            src = pl.multiple_of(jnp.minimum(src_ref[base + r], last_tok) * spt, spt)
            g_scr[slot, r * spt:(r + 1) * spt, :] = xp_ref[pl.ds(src, spt), :]

    @pl.when(i == 0)
    def _():
        gather(0, 0)

    @pl.when(i < nv)
    def _():
        slot = i % 2
        los, his = [], []
        for s in range(spt):
            lo, hi = _unpack_bf16_pair(g_scr[slot, pl.ds(s, tm, stride=spt), :])
            los.append(lo)
            his.append(hi)
        xt = jnp.concatenate(los + his, axis=1).astype(BF16)
        gather(jnp.minimum(i + 1, nt - 1), 1 - slot)
        h = jnp.dot(xt, wgu_ref[0, 0], preferred_element_type=F32) + bgu_ref[0, 0]
        g = jnp.minimum(h[:, :F], SWIGLU_LIMIT)
        u = jnp.clip(h[:, F:], -SWIGLU_LIMIT, SWIGLU_LIMIT)
        act_ref[...] = ((u + 1.0) * g * jax.nn.sigmoid(SWIGLU_ALPHA * g)).astype(BF16)

    @pl.when(i >= nv)
    def _():
        act_ref[...] = jnp.zeros_like(act_ref)


def _moe_up_call(tile_expert, n_valid, src_row, xp, w_gu, b_gu, l):
    L, E, D, F2 = w_gu.shape
    spt = (D // 2) // LANES
    tm = MOE_TILE
    nt = tile_expert.shape[0]
    grid_spec = pltpu.PrefetchScalarGridSpec(
        num_scalar_prefetch=3,
        grid=(nt,),
        in_specs=[
            pl.BlockSpec(xp.shape, lambda i, te, nv, src: (0, 0), pipeline_mode=pl.Buffered(1)),
            pl.BlockSpec((1, 1, D, F2), lambda i, te, nv, src: (l, te[i], 0, 0)),
            pl.BlockSpec((1, 1, 1, F2), lambda i, te, nv, src: (l, te[i], 0, 0)),
        ],
        out_specs=pl.BlockSpec((tm, F2 // 2), lambda i, te, nv, src: (i, 0)),
        scratch_shapes=[
            pltpu.VMEM((2, tm * spt, LANES), I32),
        ],
    )
    return pl.pallas_call(
        functools.partial(_moe_up_kernel, spt=spt),
        grid_spec=grid_spec,
        out_shape=jax.ShapeDtypeStruct((nt * tm, F2 // 2), BF16),
        compiler_params=_params(58, ("arbitrary",)),
        name="moe_up",
    )(tile_expert, n_valid, src_row, xp, w_gu, b_gu.reshape(L, E, 1, F2))


def _moe_down_kernel(te_ref, nv_ref, dst_ref, act_ref, gate_ref, wd_ref, bd_ref, o_hbm,
                     o_scr, ylo_scr, yhi_scr, sem):
    tm = act_ref.shape[0]
    spt = ylo_scr.shape[0] // tm
    half = spt * LANES
    i = pl.program_id(0)
    last = pl.num_programs(0) - 1
    nv = nv_ref[0]

    @pl.when(i == 0)
    def _():
        o_scr[...] = jnp.zeros_like(o_scr)

    def matmul():
        y = jnp.dot(act_ref[...], wd_ref[0, 0].astype(BF16), preferred_element_type=F32)
        gate = jnp.broadcast_to(gate_ref[0], (LANES, tm)).T
        for s in range(spt):
            for scr, c0 in ((ylo_scr, s * LANES), (yhi_scr, half + s * LANES)):
                cols = slice(c0, c0 + LANES)
                scr[pl.ds(s, tm, stride=spt), :] = (y[:, cols] + bd_ref[0, 0, :, cols]) * gate

    def accumulate(tile):
        base = tile * tm
        for c0 in range(0, tm, ACC_CHUNK):
            dsts = [pl.multiple_of(dst_ref[base + r] * spt, spt)
                    for r in range(c0, c0 + ACC_CHUNK)]
            sums = []
            for k, r in enumerate(range(c0, c0 + ACC_CHUNK)):
                rows = slice(r * spt, (r + 1) * spt)
                lo, hi = _unpack_bf16_pair(o_scr[pl.ds(dsts[k], spt), :])
                sums.append(_pack_bf16_pair(lo + ylo_scr[rows, :], hi + yhi_scr[rows, :]))
            for k in range(ACC_CHUNK):
                o_scr[pl.ds(dsts[k], spt), :] = sums[k]

    @pl.when(jnp.logical_and(i == 0, i < nv))
    def _():
        matmul()

    @pl.when(jnp.logical_and(i >= 1, i < nv))
    def _():
        accumulate(i - 1)
        matmul()

    @pl.when(jnp.logical_and(i >= 1, i == nv))
    def _():
        accumulate(i - 1)

    @pl.when(i == last)
    def _():
        out = pltpu.make_async_copy(o_scr, o_hbm, sem)
        out.start()
        out.wait()


def _moe_down_call(tile_expert, n_valid, dst_row, gate_sorted, act, w_down, b_down, l,
                   n_tokens):
    L, E, F, D = w_down.shape
    spt = (D // 2) // LANES
    tm = MOE_TILE
    nt = tile_expert.shape[0]
    n_rows = (n_tokens + 1) * spt
    tile = lambda i: jnp.minimum(i, nt - 1)
    grid_spec = pltpu.PrefetchScalarGridSpec(
        num_scalar_prefetch=3,
        grid=(nt + 1,),
        in_specs=[
            pl.BlockSpec((tm, F), lambda i, te, nv, dst: (tile(i), 0)),
            pl.BlockSpec((1, 1, tm), lambda i, te, nv, dst: (tile(i), 0, 0)),
            pl.BlockSpec((1, 1, F, D), lambda i, te, nv, dst: (l, te[tile(i)], 0, 0)),
            pl.BlockSpec((1, 1, 1, D), lambda i, te, nv, dst: (l, te[tile(i)], 0, 0)),
        ],
        out_specs=pl.BlockSpec(memory_space=pl.ANY),
        scratch_shapes=[
            pltpu.VMEM((n_rows, LANES), I32),
            pltpu.VMEM((tm * spt, LANES), F32),
            pltpu.VMEM((tm * spt, LANES), F32),
            pltpu.SemaphoreType.DMA(()),
        ],
    )
    return pl.pallas_call(
        _moe_down_kernel,
        grid_spec=grid_spec,
        out_shape=jax.ShapeDtypeStruct((n_rows, LANES), I32),
        compiler_params=_params(58, ("arbitrary",)),
        name="moe_down",
    )(tile_expert, n_valid, dst_row, act, gate_sorted.reshape(nt, 1, tm), w_down,
      b_down.reshape(L, E, 1, D))


def _route_invert_kernel(dest_ref, gate_ref, tok_fill, gate_fill, tok_ref, gsort_ref, sem):
    n_pairs = dest_ref.shape[0]
    fills = [pltpu.make_async_copy(tok_fill, tok_ref, sem.at[0]),
             pltpu.make_async_copy(gate_fill, gsort_ref, sem.at[1])]
    for f in fills:
        f.start()
    for f in fills:
        f.wait()

    per_step = ROUTE_UNROLL // TOP_K_EXPERTS

    def place(b, c):
        for q in range(ROUTE_UNROLL):
            p = b * ROUTE_UNROLL + q
            d = dest_ref[p]
            tok_ref[d] = b * per_step + q // TOP_K_EXPERTS
            gsort_ref[d] = gate_ref[p]
        return c

    lax.fori_loop(0, n_pairs // ROUTE_UNROLL, place, 0)


def _route_invert_call(dest, gate, n_slots, n_tokens):
    smem = pl.BlockSpec(memory_space=pltpu.SMEM)
    hbm = pl.BlockSpec(memory_space=pl.ANY)
    return pl.pallas_call(
        _route_invert_kernel,
        in_specs=[smem, smem, hbm, hbm],
        out_specs=[smem, smem],
        out_shape=[jax.ShapeDtypeStruct((n_slots,), I32),
                   jax.ShapeDtypeStruct((n_slots,), F32)],
        scratch_shapes=[pltpu.SemaphoreType.DMA((2,))],
        name="route_invert",
    )(dest, gate, jnp.full((n_slots,), n_tokens, I32), jnp.zeros((n_slots,), F32))


def _moe_routing(meta, counts, n_tokens, n_experts):
    tm = MOE_TILE
    n_pairs = TOP_K_EXPERTS * n_tokens
    nt = n_pairs // tm + n_experts
    ei = meta[0:TOP_K_EXPERTS].T
    tw = lax.bitcast_convert_type(meta[TOP_K_EXPERTS:2 * TOP_K_EXPERTS], F32).T
    rk = meta[2 * TOP_K_EXPERTS:3 * TOP_K_EXPERTS].T
    cnt = counts[:, 0]
    tiles = (cnt + tm - 1) // tm
    tile_end = jnp.cumsum(tiles)
    group_start = (tile_end - tiles) * tm
    starts = jnp.sum(jnp.where(ei[..., None] == jnp.arange(n_experts, dtype=I32),
                               group_start.astype(I32), 0), axis=-1)
    dest = (starts + rk).reshape(-1)
    tok_sorted, gate_sorted = _route_invert_call(dest, tw.reshape(-1), nt * tm, n_tokens)
    n_valid = tile_end[-1:].astype(I32)
    tid = jnp.minimum(jnp.arange(nt, dtype=I32), n_valid[0] - 1)
    tile_expert = jnp.sum((tile_end[None, :] <= tid[:, None]).astype(I32), axis=1)
    return tile_expert, n_valid, tok_sorted, gate_sorted


def _moe_res_ln_kernel(o_ref_in, x1_ref, g2_ref, lg_ref, lb_ref, o_ref, *, alpha):
    tc = x1_ref.shape[0]
    spt = o_ref_in.shape[0] // tc
    los, his = [], []
    for s in range(spt):
        lo, hi = _unpack_bf16_pair(o_ref_in[pl.ds(s, tc, stride=spt), :])
        los.append(lo)
        his.append(hi)
    moe = jnp.concatenate(los + his, axis=1)
    r = alpha * x1_ref[...] + (1.0 + g2_ref[0]) * moe
    o_ref[...] = _layer_norm(r, lg_ref[0], lb_ref[0])


def _moe_res_ln_call(o_packed, x1, mods, g2_row, ln, ln_row, S, alpha):
    T, D = x1.shape
    spt = (D // 2) // LANES
    tc = ROW_TILE
    tpb = S // tc
    lnspec = pl.BlockSpec((1, 1, D), lambda i: (ln_row, 0, 0))
    return pl.pallas_call(
        functools.partial(_moe_res_ln_kernel, alpha=alpha),
        grid=(T // tc,),
        in_specs=[pl.BlockSpec((tc * spt, LANES), lambda i: (i, 0)),
                  pl.BlockSpec((tc, D), lambda i: (i, 0)),
                  pl.BlockSpec((1, 1, D), lambda i: (g2_row(i // tpb), 0, 0)),
                  lnspec, lnspec],
        out_specs=pl.BlockSpec((tc, D), lambda i: (i, 0)),
        out_shape=jax.ShapeDtypeStruct((T, D), F32),
        compiler_params=_params(40, ("arbitrary",)),
        name="moe_res_ln",
    )(o_packed, x1, mods, ln[0], ln[1])


def _q_proj_kernel(x_ref, sh_ref, sc_ref, wt_ref, o_ref, *, out_scale):
    h = (x_ref[...] * (1.0 + sc_ref[0]) + sh_ref[0]).astype(BF16)
    zt = lax.dot_general(wt_ref[0], h, (((1,), (1,)), ((), ())), preferred_element_type=F32)
    o_ref[0] = (zt * out_scale).astype(BF16)


def _q_proj_call(x, mods, sh_row, sc_row, w_t, l, S, out_scale):
    T, D = x.shape
    N = w_t.shape[1]
    tm = MOBA_BLOCK
    tpb = S // tm
    mod = lambda row: pl.BlockSpec((1, 1, D), lambda i: (row(i // tpb), 0, 0))
    return pl.pallas_call(
        functools.partial(_q_proj_kernel, out_scale=out_scale),
        grid=(T // tm,),
        in_specs=[pl.BlockSpec((tm, D), lambda i: (i, 0)), mod(sh_row), mod(sc_row),
                  _layer_resident((N, D), l)],
        out_specs=pl.BlockSpec((1, N, tm), lambda i: (i, 0, 0)),
        out_shape=jax.ShapeDtypeStruct((T // tm, N, tm), BF16),
        compiler_params=_params(48, ("arbitrary",)),
        name="q_proj",
    )(x, mods, mods, w_t)


def _kv_proj_kernel(x_ref, sh_ref, sc_ref, wk_ref, wvt_ref, k_ref, vt_ref, km_ref):
    h = (x_ref[...] * (1.0 + sc_ref[0]) + sh_ref[0]).astype(BF16)
    zk = jnp.dot(h, wk_ref[...], preferred_element_type=F32)
    k_ref[...] = zk.astype(BF16)
    km_ref[0] = jnp.mean(zk, axis=0, keepdims=True)
    zvt = lax.dot_general(wvt_ref[...], h, (((1,), (1,)), ((), ())),
                          preferred_element_type=F32)
    vt_ref[0] = zvt.astype(BF16)


def _kv_proj_call(x, mods, sh_row, sc_row, w_k, w_vt, S):
    T, D = x.shape
    HD = w_k.shape[1]
    tm = MOBA_BLOCK
    tpb = S // tm
    mod = lambda row: pl.BlockSpec((1, 1, D), lambda i: (row(i // tpb), 0, 0))
    return pl.pallas_call(
        _kv_proj_kernel,
        grid=(T // tm,),
        in_specs=[pl.BlockSpec((tm, D), lambda i: (i, 0)), mod(sh_row), mod(sc_row),
                  _resident((D, HD)), _resident((HD, D))],
        out_specs=[pl.BlockSpec((tm, HD), lambda i: (i, 0)),
                   pl.BlockSpec((1, HD, tm), lambda i: (i, 0, 0)),
                   pl.BlockSpec((1, 1, HD), lambda i: (i, 0, 0))],
        out_shape=[jax.ShapeDtypeStruct((T, HD), BF16),
                   jax.ShapeDtypeStruct((T // tm, HD, tm), BF16),
                   jax.ShapeDtypeStruct((T // tm, 1, HD), F32)],
        compiler_params=_params(48, ("arbitrary",)),
        name="kv_proj",
    )(x, mods, mods, w_k, w_vt)


def _moba_kernel(qt_ref, k_ref, vt_ref, km_ref, o_ref, m_scr, l_scr, acc_scr, s_scr, *, hps):
    bs = qt_ref.shape[2]
    nb = km_ref.shape[1]
    qb = pl.program_id(2)
    start = pl.multiple_of(qb * bs, bs)
    kidx = lax.broadcasted_iota(I32, (bs, bs), 0)
    qidx = lax.broadcasted_iota(I32, (bs, bs), 1)
    blk = lax.broadcasted_iota(I32, (nb, bs), 0)
    valid = blk < qb
    heads = [slice(h * HEAD_DIM, (h + 1) * HEAD_DIM) for h in range(hps)]

    biases = []
    for cs in heads:
        qt = qt_ref[0, cs, :]
        km = km_ref[0, :, cs]
        km_hi = km.astype(BF16)
        km_lo = (km - km_hi.astype(F32)).astype(BF16)
        gate = (jnp.dot(km_hi, qt, preferred_element_type=F32)
                + jnp.dot(km_lo, qt, preferred_element_type=F32))
        gate = jnp.where(valid, gate, -jnp.inf)
        beaten = jnp.zeros((nb, bs), F32)
        for jp in range(nb):
            row = gate[jp:jp + 1, :]
            wins = jnp.logical_or(row > gate, jnp.logical_and(row == gate, jp < blk))
            beaten = beaten + jnp.where(wins, 1.0, 0.0)
        biases.append(jnp.where(jnp.logical_and(valid, beaten < MOBA_TOPK), 0.0, MASK_VALUE))

    def scores(rows, slot):
        for h, cs in enumerate(heads):
            s_scr[slot, h] = jnp.dot(k_ref[rows, cs], qt_ref[0, cs, :],
                                     preferred_element_type=F32)

    own_slot = 1
    scores(pl.ds(start, bs), own_slot)
    scores(slice(0, bs), 0)

    for h, cs in enumerate(heads):
        s = jnp.where(kidx <= qidx, s_scr[own_slot, h], MASK_VALUE)
        m = jnp.max(s, axis=0, keepdims=True)
        p = jnp.exp2(s - m)
        m_scr[h] = m
        l_scr[h] = jnp.sum(p, axis=0, keepdims=True)
        acc_scr[h] = jnp.dot(vt_ref[qb, cs, :], p.astype(BF16), preferred_element_type=F32)

    for j in range(nb - 1):
        @pl.when(j < qb)
        def _(j=j):
            if j + 1 < nb - 1:
                scores(slice((j + 1) * bs, (j + 2) * bs), (j + 1) % 2)
            for h, cs in enumerate(heads):
                bias = biases[h][j:j + 1, :]
                m_old = m_scr[h]
                m_new = jnp.maximum(
                    m_old, jnp.max(s_scr[j % 2, h], axis=0, keepdims=True) + bias)
                a = jnp.exp2(m_old - m_new)
                shift = m_new - bias
                p = jnp.exp2(s_scr[j % 2, h] - shift)
                l_scr[h] = a * l_scr[h] + jnp.sum(p, axis=0, keepdims=True)
                acc_scr[h] = a * acc_scr[h] + jnp.dot(
                    vt_ref[j, cs, :], p.astype(BF16), preferred_element_type=F32)
                m_scr[h] = m_new

    for h, cs in enumerate(heads):
        o_ref[:, cs] = (acc_scr[h] * (1.0 / l_scr[h])).T.astype(BF16)


def _moba_call(qt, k, vt, kmeans, B, S, hps):
    T, HD = k.shape
    H = HD // HEAD_DIM
    bs = MOBA_BLOCK
    nb = S // bs
    hw = hps * HEAD_DIM
    return pl.pallas_call(
        functools.partial(_moba_kernel, hps=hps),
        grid=(B, H // hps, nb),
        in_specs=[
            pl.BlockSpec((1, hw, bs), lambda b, g, i: (b * nb + i, g, 0)),
            pl.BlockSpec((S, hw), lambda b, g, i: (b, g)),
            pl.BlockSpec((nb, hw, bs), lambda b, g, i: (b, g, 0)),
            pl.BlockSpec((1, nb, hw), lambda b, g, i: (b, 0, g)),
        ],
        out_specs=pl.BlockSpec((bs, hw), lambda b, g, i: (b * nb + i, g)),
        out_shape=jax.ShapeDtypeStruct((T, HD), BF16),
        scratch_shapes=[pltpu.VMEM((hps, 1, bs), F32), pltpu.VMEM((hps, 1, bs), F32),
                        pltpu.VMEM((hps, HEAD_DIM, bs), F32),
                        pltpu.VMEM((2, hps, bs, bs), F32)],
        compiler_params=_params(48, ("arbitrary", "arbitrary", "arbitrary")),
        name="moba_attention",
    )(qt, k, vt, kmeans.reshape(B, nb, HD))


def kernel(x, c, ada_w, ada_b, ln_g, ln_b, gm_w_in, gm_b_in, gm_lnv_g, gm_lnv_b, gm_w_s,
           gm_b_s, gm_w_out, kv_ada_w, kv_ada_b, w_kv, attn_w_q, attn_w_out, moe_w_router,
           moe_b_router, moe_w_gu, moe_b_gu, moe_w_down, moe_b_down):
    B, S, D = x.shape
    T = B * S
    depth = ada_w.shape[0]
    n_a = gm_w_in.shape[0]
    E = moe_w_router.shape[2]
    W = gm_w_in.shape[2] // 2
    C = gm_w_s.shape[2]
    H = attn_w_q.shape[2] // HEAD_DIM
    hps = next(n for n in (8, 4, 2, 1) if H % n == 0)
    assert B <= MODS_PAD_ROWS and S % MOBA_BLOCK == 0 and S % ROW_TILE == 0
    assert ROW_TILE % C == 0 and W // gm_w_s.shape[1] == LANES
    assert (D // 2) % LANES == 0 and (TOP_K_EXPERTS * T) % MOE_TILE == 0
    alpha = (2.0 * depth) ** 0.25

    c_pad = jnp.zeros((MODS_PAD_ROWS, D), F32).at[:B].set(c)
    n_mod = ada_w.shape[2] // D
    mods = _mods_call(c_pad, ada_w, ada_b).reshape(depth * MODS_PAD_ROWS * n_mod, 1, D)
    kv_mods = _mods_call(c_pad, kv_ada_w[None], kv_ada_b[None]).reshape(
        MODS_PAD_ROWS * 2, 1, D)

    def mod_row(l, which):
        return lambda b: (l * MODS_PAD_ROWS + b) * n_mod + which

    ln_g3 = ln_g.reshape(depth * 2, 1, D)
    ln_b3 = ln_b.reshape(depth * 2, 1, D)
    causal = jnp.tril(jnp.ones((C, C), F32))
    gm_w_in_b = gm_w_in.astype(BF16)
    gm_w_out_b = gm_w_out.astype(BF16)
    gm_ws_b = (gm_w_s * causal).astype(BF16)
    gm_bs_t = jnp.swapaxes(gm_b_s, 1, 2)
    attn_w_qt_b = jnp.swapaxes(attn_w_q, 1, 2).astype(BF16)
    attn_w_out_b = attn_w_out.astype(BF16)
    w_rt_b = jnp.swapaxes(moe_w_router, 1, 2).astype(BF16)
    w_gu_b = moe_w_gu.astype(BF16)

    xt = x.reshape(T, D)
    k = vt = kmeans = None
    for l in range(depth):
        if l < n_a:
            y = _gmlp_in_call(
                xt, mods, mod_row(l, 0), mod_row(l, 1), gm_w_in_b, gm_b_in[:, None],
                gm_lnv_g[:, None], gm_lnv_b[:, None], gm_ws_b, gm_bs_t, l, S)
            w_o, lw = gm_w_out_b, l
        else:
            lw = l - n_a
            qt = _q_proj_call(xt, mods, mod_row(l, 0), mod_row(l, 1), attn_w_qt_b, lw, S,
                              HEAD_DIM ** -0.5 * LOG2_E)
            y = _moba_call(qt, k, vt, kmeans, B, S, hps)
            w_o = attn_w_out_b
        x1, xp, meta, counts = _proj_ln_router_call(
            y, w_o, lw, xt, mods, mod_row(l, 2), mod_row(l, 3), mod_row(l, 4),
            (ln_g3, ln_b3), 2 * l, w_rt_b, moe_b_router[:, :, None], l, S, alpha)
        tile_expert, n_valid, tok_sorted, gate_sorted = _moe_routing(meta, counts, T, E)
        act = _moe_up_call(tile_expert, n_valid, tok_sorted, xp, w_gu_b, moe_b_gu, l)
        moe_out = _moe_down_call(tile_expert, n_valid, tok_sorted, gate_sorted, act,
                                 moe_w_down, moe_b_down, l, T)
        xt = _moe_res_ln_call(moe_out, x1, mods, mod_row(l, 5), (ln_g3, ln_b3), 2 * l + 1, S,
                              alpha)
        if l == n_a - 1:
            hd = w_kv.shape[1] // 2
            k, vt, kmeans = _kv_proj_call(
                xt, kv_mods, lambda b: b * 2, lambda b: b * 2 + 1,
                w_kv[:, :hd].astype(BF16), w_kv[:, hd:].T.astype(BF16), S)
    return xt.reshape(B, S, D)
```

```python
import functools

import jax
import jax.numpy as jnp
from jax import lax
from jax.experimental import pallas as pl
from jax.experimental.pallas import tpu as pltpu

HEAD_DIM = 128
MOBA_BLOCK = 256
MOBA_TOPK = 3
TOP_K_EXPERTS = 4
SWIGLU_LIMIT = 7.0
SWIGLU_ALPHA = 1.702
LN_EPS = 1e-5

LANES = 128
SUBLANES = 8

ROW_TILE = 512
MOE_TILE = 512
ACC_CHUNK = 16
ROUTE_UNROLL = 16
META_ROWS = 16
MODS_PAD_ROWS = SUBLANES
MASK_VALUE = -1e30
LOG2_E = 1.4426950408889634
HI16 = -65536

F32 = jnp.float32
BF16 = jnp.bfloat16
I32 = jnp.int32


def _params(vmem_mb, semantics):
    return pltpu.CompilerParams(dimension_semantics=semantics,
                                vmem_limit_bytes=vmem_mb * 1024 * 1024)


def _resident(shape):
    nd = len(shape)
    return pl.BlockSpec(shape, lambda *_: (0,) * nd, pipeline_mode=pl.Buffered(1))


def _layer_resident(tail, l):
    nd = len(tail)
    return pl.BlockSpec((1,) + tuple(tail), lambda *_: (l,) + (0,) * nd,
                        pipeline_mode=pl.Buffered(1))


def _layer_norm(r, g, b):
    mu = jnp.mean(r, axis=-1, keepdims=True)
    d = r - mu
    var = jnp.mean(d * d, axis=-1, keepdims=True)
    return d * lax.rsqrt(var + LN_EPS) * g + b


def _pack_bf16_pair(lo, hi):
    lo = lax.bitcast_convert_type(lo.astype(BF16).astype(F32), I32)
    hi = lax.bitcast_convert_type(hi.astype(BF16).astype(F32), I32)
    return lax.shift_right_logical(lo, 16) | (hi & HI16)


def _unpack_bf16_pair(w):
    lo = lax.bitcast_convert_type(lax.shift_left(w, 16), F32)
    hi = lax.bitcast_convert_type(w & HI16, F32)
    return lo, hi


def _mods_kernel(c_ref, w_ref, b_ref, o_ref):
    c = c_ref[...]
    ca = (c * jax.nn.sigmoid(c)).astype(BF16)
    w = w_ref[0].astype(BF16)
    o_ref[0] = jnp.dot(ca, w, preferred_element_type=F32) + b_ref[0]


def _mods_call(c_pad, w, b, tn=1024):
    L, D, N = w.shape
    return pl.pallas_call(
        _mods_kernel,
        grid=(L, N // tn),
        in_specs=[
            pl.BlockSpec((MODS_PAD_ROWS, D), lambda l, j: (0, 0)),
            pl.BlockSpec((1, D, tn), lambda l, j: (l, 0, j)),
            pl.BlockSpec((1, 1, tn), lambda l, j: (l, 0, j)),
        ],
        out_specs=pl.BlockSpec((1, MODS_PAD_ROWS, tn), lambda l, j: (l, 0, j)),
        out_shape=jax.ShapeDtypeStruct((L, MODS_PAD_ROWS, N), F32),
        compiler_params=_params(40, ("arbitrary", "arbitrary")),
        name="adaln_mods",
    )(c_pad, w, b.reshape(L, 1, N))


def _gmlp_in_kernel(x_ref, sh_ref, sc_ref, w_ref, b_ref, lg_ref, lb_ref, ws_ref, bs_ref,
                    y_ref, u_scr, v_scr, vn_scr, *, ncol):
    tm, D = x_ref.shape
    W = y_ref.shape[1]
    _, G, C, _ = ws_ref.shape
    gd = W // G
    h = (x_ref[...] * (1.0 + sc_ref[0]) + sh_ref[0]).astype(BF16)
    for j in range(2 * W // ncol):
        z = jnp.dot(h, w_ref[0, :, j * ncol:(j + 1) * ncol], preferred_element_type=F32)
        z = jax.nn.gelu(z + b_ref[0, :, j * ncol:(j + 1) * ncol], approximate=True)
        if j * ncol < W:
            u_scr[:, j * ncol:(j + 1) * ncol] = z
        else:
            v_scr[:, j * ncol - W:(j + 1) * ncol - W] = z
    vn_scr[...] = _layer_norm(v_scr[...], lg_ref[0], lb_ref[0]).astype(BF16)
    for n in range(tm // C):
        rows = slice(n * C, (n + 1) * C)
        for g in range(G):
            cols = slice(g * gd, (g + 1) * gd)
            sv = jnp.dot(ws_ref[0, g], vn_scr[rows, cols], preferred_element_type=F32)
            sv = sv + bs_ref[0, :, g:g + 1]
            y_ref[rows, cols] = (u_scr[rows, cols] * sv).astype(BF16)


def _gmlp_in_call(x, mods, sh_row, sc_row, w_in, b_in, lnv_g, lnv_b, ws, bs_t, l, S):
    T, D = x.shape
    W = w_in.shape[2] // 2
    _, G, C, _ = ws.shape
    tm = ROW_TILE
    tpb = S // tm
    mod = lambda row: pl.BlockSpec((1, 1, D), lambda i: (row(i // tpb), 0, 0))
    return pl.pallas_call(
        functools.partial(_gmlp_in_kernel, ncol=512),
        grid=(T // tm,),
        in_specs=[
            pl.BlockSpec((tm, D), lambda i: (i, 0)),
            mod(sh_row), mod(sc_row),
            _layer_resident((D, 2 * W), l),
            _layer_resident((1, 2 * W), l),
            _layer_resident((1, W), l), _layer_resident((1, W), l),
            _layer_resident((G, C, C), l),
            _layer_resident((C, G), l),
        ],
        out_specs=pl.BlockSpec((tm, W), lambda i: (i, 0)),
        out_shape=jax.ShapeDtypeStruct((T, W), BF16),
        scratch_shapes=[pltpu.VMEM((tm, W), F32), pltpu.VMEM((tm, W), F32),
                        pltpu.VMEM((tm, W), BF16)],
        compiler_params=_params(48, ("arbitrary",)),
        name="gmlp_in",
    )(x, mods, mods, w_in, b_in, lnv_g, lnv_b, ws, bs_t)


def _proj_ln_router_kernel(a_ref, w_ref, x_ref, g1_ref, lg_ref, lb_ref, sh_ref, sc_ref,
                           wr_ref, br_ref,
                           x1_ref, xp_ref, meta_ref, cnt_ref, carry, *, alpha):
    tm, D = x_ref.shape
    E = wr_ref.shape[1]
    spt = xp_ref.shape[0] // tm
    i = pl.program_id(0)

    @pl.when(i == 0)
    def _():
        carry[...] = jnp.zeros_like(carry)

    h = jnp.dot(a_ref[...], w_ref[0], preferred_element_type=F32)
    x1 = _layer_norm(alpha * x_ref[...] + (1.0 + g1_ref[0]) * h, lg_ref[0], lb_ref[0])
    x1_ref[...] = x1
    xm = x1 * (1.0 + sc_ref[0]) + sh_ref[0]
    pk = _pack_bf16_pair(xm[:, :D // 2], xm[:, D // 2:])
    for s in range(spt):
        xp_ref[pl.ds(s, tm, stride=spt), :] = pk[:, s * LANES:(s + 1) * LANES]

    logits = lax.dot_general(wr_ref[0], xm.astype(BF16), (((1,), (1,)), ((), ())),
                             preferred_element_type=F32) + br_ref[0]
    sub = lax.broadcasted_iota(I32, (E, tm), 0).astype(F32)
    onehots, vals, idxs = [], [], []
    cur = logits
    for _k in range(TOP_K_EXPERTS):
        m = jnp.max(cur, axis=0, keepdims=True)
        idx = jnp.min(jnp.where(cur == m, sub, float(E)), axis=0, keepdims=True)
        oh = sub == idx
        cur = jnp.where(oh, -jnp.inf, cur)
        onehots.append(jnp.where(oh, 1.0, 0.0))
        vals.append(m)
        idxs.append(idx.astype(I32))
    exps = [jnp.exp(v - vals[0]) for v in vals]
    denom = exps[0]
    for e in exps[1:]:
        denom = denom + e
    oh_all = onehots[0]
    for oh in onehots[1:]:
        oh_all = oh_all + oh
    r_i = lax.broadcasted_iota(I32, (tm, tm), 0)
    c_i = lax.broadcasted_iota(I32, (tm, tm), 1)
    earlier = jnp.where(r_i < c_i, 1.0, 0.0).astype(BF16)
    before = jnp.dot(oh_all.astype(BF16), earlier, preferred_element_type=F32) + carry[...]
    row = lax.broadcasted_iota(I32, (META_ROWS, tm), 0)
    meta = jnp.zeros((META_ROWS, tm), I32)
    for k in range(TOP_K_EXPERTS):
        rank = jnp.sum(onehots[k] * before, axis=0, keepdims=True).astype(I32)
        wk = lax.bitcast_convert_type(exps[k] / denom, I32)
        meta = jnp.where(row == k, idxs[k], meta)
        meta = jnp.where(row == TOP_K_EXPERTS + k, wk, meta)
        meta = jnp.where(row == 2 * TOP_K_EXPERTS + k, rank, meta)
    meta_ref[...] = meta
    carry[...] = carry[...] + jnp.sum(oh_all, axis=1, keepdims=True)
    cnt_ref[...] = carry[...].astype(I32)


def _proj_ln_router_call(a, w, lw, x, mods, g1_row, sh_row, sc_row, ln, ln_row, w_r, b_r, l,
                         S, alpha):
    T, D = x.shape
    K = a.shape[1]
    E = w_r.shape[1]
    spt = (D // 2) // LANES
    tm = ROW_TILE
    tpb = S // tm
    mod = lambda row: pl.BlockSpec((1, 1, D), lambda i: (row(i // tpb), 0, 0))
    lnspec = pl.BlockSpec((1, 1, D), lambda i: (ln_row, 0, 0))
    return pl.pallas_call(
        functools.partial(_proj_ln_router_kernel, alpha=alpha),
        grid=(T // tm,),
        in_specs=[
            pl.BlockSpec((tm, K), lambda i: (i, 0)),
            _layer_resident((K, D), lw),
            pl.BlockSpec((tm, D), lambda i: (i, 0)),
            mod(g1_row), lnspec, lnspec,
            mod(sh_row), mod(sc_row),
            _layer_resident((E, D), l), _layer_resident((E, 1), l),
        ],
        out_specs=[
            pl.BlockSpec((tm, D), lambda i: (i, 0)),
            pl.BlockSpec((tm * spt, LANES), lambda i: (i, 0)),
            pl.BlockSpec((META_ROWS, tm), lambda i: (0, i)),
            pl.BlockSpec((E, 1), lambda i: (0, 0)),
        ],
        out_shape=[
            jax.ShapeDtypeStruct((T, D), F32),
            jax.ShapeDtypeStruct((T * spt, LANES), I32),
            jax.ShapeDtypeStruct((META_ROWS, T), I32),
            jax.ShapeDtypeStruct((E, 1), I32),
        ],
        scratch_shapes=[pltpu.VMEM((E, 1), F32)],
        compiler_params=_params(48, ("arbitrary",)),
        name="proj_ln_router",
    )(a, w, x, mods, ln[0], ln[1], mods, mods, w_r, b_r)


def _moe_up_kernel(te_ref, nv_ref, src_ref, xp_ref, wgu_ref, bgu_ref, act_ref,
                   g_scr, *, spt):
    tm, F = act_ref.shape
    i = pl.program_id(0)
    nt = pl.num_programs(0)
    nv = nv_ref[0]

    def gather(tile, slot):
        base = tile * tm
        last_tok = xp_ref.shape[0] // spt - 1
        for r in range(tm):
            src = pl.multiple_of(jnp.minimum(src_ref[base + r], last_tok) * spt, spt)
            g_scr[slot, r * spt:(r + 1) * spt, :] = xp_ref[pl.ds(src, spt), :]

    @pl.when(i == 0)
    def _():
        gather(0, 0)

    @pl.when(i < nv)
    def _():
        slot = i % 2
        los, his = [], []
        for s in range(spt):
            lo, hi = _unpack_bf16_pair(g_scr[slot, pl.ds(s, tm, stride=spt), :])
            los.append(lo)
            his.append(hi)
        xt = jnp.concatenate(los + his, axis=1).astype(BF16)
        gather(jnp.minimum(i + 1, nt - 1), 1 - slot)
        h = jnp.dot(xt, wgu_ref[0, 0], preferred_element_type=F32) + bgu_ref[0, 0]
        g = jnp.minimum(h[:, :F], SWIGLU_LIMIT)
        u = jnp.clip(h[:, F:], -SWIGLU_LIMIT, SWIGLU_LIMIT)
        act_ref[...] = ((u + 1.0) * g * jax.nn.sigmoid(SWIGLU_ALPHA * g)).astype(BF16)

    @pl.when(i >= nv)
    def _():
        act_ref[...] = jnp.zeros_like(act_ref)


def _moe_up_call(tile_expert, n_valid, src_row, xp, w_gu, b_gu, l):
    L, E, D, F2 = w_gu.shape
    spt = (D // 2) // LANES
    tm = MOE_TILE
    nt = tile_expert.shape[0]
    grid_spec = pltpu.PrefetchScalarGridSpec(
        num_scalar_prefetch=3,
        grid=(nt,),
        in_specs=[
            pl.BlockSpec(xp.shape, lambda i, te, nv, src: (0, 0), pipeline_mode=pl.Buffered(1)),
            pl.BlockSpec((1, 1, D, F2), lambda i, te, nv, src: (l, te[i], 0, 0)),
            pl.BlockSpec((1, 1, 1, F2), lambda i, te, nv, src: (l, te[i], 0, 0)),
        ],
        out_specs=pl.BlockSpec((tm, F2 // 2), lambda i, te, nv, src: (i, 0)),
        scratch_shapes=[
            pltpu.VMEM((2, tm * spt, LANES), I32),
        ],
    )
    return pl.pallas_call(
        functools.partial(_moe_up_kernel, spt=spt),
        grid_spec=grid_spec,
        out_shape=jax.ShapeDtypeStruct((nt * tm, F2 // 2), BF16),
        compiler_params=_params(58, ("arbitrary",)),
        name="moe_up",
    )(tile_expert, n_valid, src_row, xp, w_gu, b_gu.reshape(L, E, 1, F2))


def _moe_down_kernel(te_ref, nv_ref, dst_ref, act_ref, gate_ref, wd_ref, bd_ref, o_hbm,
                     o_scr, ylo_scr, yhi_scr, sem):
    tm = act_ref.shape[0]
    spt = ylo_scr.shape[0] // tm
    half = spt * LANES
    i = pl.program_id(0)
    last = pl.num_programs(0) - 1
    nv = nv_ref[0]

    @pl.when(i == 0)
    def _():
        o_scr[...] = jnp.zeros_like(o_scr)

    def matmul():
        y = jnp.dot(act_ref[...], wd_ref[0, 0].astype(BF16), preferred_element_type=F32)
        gate = jnp.broadcast_to(gate_ref[0], (LANES, tm)).T
        for s in range(spt):
            for scr, c0 in ((ylo_scr, s * LANES), (yhi_scr, half + s * LANES)):
                cols = slice(c0, c0 + LANES)
                scr[pl.ds(s, tm, stride=spt), :] = (y[:, cols] + bd_ref[0, 0, :, cols]) * gate

    def accumulate(tile):
        base = tile * tm
        for c0 in range(0, tm, ACC_CHUNK):
            dsts = [pl.multiple_of(dst_ref[base + r] * spt, spt)
                    for r in range(c0, c0 + ACC_CHUNK)]
            sums = []
            for k, r in enumerate(range(c0, c0 + ACC_CHUNK)):
                rows = slice(r * spt, (r + 1) * spt)
                lo, hi = _unpack_bf16_pair(o_scr[pl.ds(dsts[k], spt), :])
                sums.append(_pack_bf16_pair(lo + ylo_scr[rows, :], hi + yhi_scr[rows, :]))
            for k in range(ACC_CHUNK):
                o_scr[pl.ds(dsts[k], spt), :] = sums[k]

    @pl.when(jnp.logical_and(i == 0, i < nv))
    def _():
        matmul()

    @pl.when(jnp.logical_and(i >= 1, i < nv))
    def _():
        accumulate(i - 1)
        matmul()

    @pl.when(jnp.logical_and(i >= 1, i == nv))
    def _():
        accumulate(i - 1)

    @pl.when(i == last)
    def _():
        out = pltpu.make_async_copy(o_scr, o_hbm, sem)
        out.start()
        out.wait()


def _moe_down_call(tile_expert, n_valid, dst_row, gate_sorted, act, w_down, b_down, l,
                   n_tokens):
    L, E, F, D = w_down.shape
    spt = (D // 2) // LANES
    tm = MOE_TILE
    nt = tile_expert.shape[0]
    n_rows = (n_tokens + 1) * spt
    tile = lambda i: jnp.minimum(i, nt - 1)
    grid_spec = pltpu.PrefetchScalarGridSpec(
        num_scalar_prefetch=3,
        grid=(nt + 1,),
        in_specs=[
            pl.BlockSpec((tm, F), lambda i, te, nv, dst: (tile(i), 0)),
            pl.BlockSpec((1, 1, tm), lambda i, te, nv, dst: (tile(i), 0, 0)),
            pl.BlockSpec((1, 1, F, D), lambda i, te, nv, dst: (l, te[tile(i)], 0, 0)),
            pl.BlockSpec((1, 1, 1, D), lambda i, te, nv, dst: (l, te[tile(i)], 0, 0)),
        ],
        out_specs=pl.BlockSpec(memory_space=pl.ANY),
        scratch_shapes=[
            pltpu.VMEM((n_rows, LANES), I32),
            pltpu.VMEM((tm * spt, LANES), F32),
            pltpu.VMEM((tm * spt, LANES), F32),
            pltpu.SemaphoreType.DMA(()),
        ],
    )
    return pl.pallas_call(
        _moe_down_kernel,
        grid_spec=grid_spec,
        out_shape=jax.ShapeDtypeStruct((n_rows, LANES), I32),
        compiler_params=_params(58, ("arbitrary",)),
        name="moe_down",
    )(tile_expert, n_valid, dst_row, act, gate_sorted.reshape(nt, 1, tm), w_down,
      b_down.reshape(L, E, 1, D))


def _route_invert_kernel(dest_ref, val_ref, fill_hbm, out_ref, sem):
    n_pairs = dest_ref.shape[0]
    fill = pltpu.make_async_copy(fill_hbm, out_ref, sem)
    fill.start()
    fill.wait()

    def place(b, c):
        for q in range(ROUTE_UNROLL):
            p = b * ROUTE_UNROLL + q
            out_ref[dest_ref[p]] = val_ref[p]
        return c

    lax.fori_loop(0, n_pairs // ROUTE_UNROLL, place, 0)


def _route_invert_call(dest, vals, n_slots, fill_value):
    smem = pl.BlockSpec(memory_space=pltpu.SMEM)
    return pl.pallas_call(
        _route_invert_kernel,
        in_specs=[smem, smem, pl.BlockSpec(memory_space=pl.ANY)],
        out_specs=smem,
        out_shape=jax.ShapeDtypeStruct((n_slots,), I32),
        scratch_shapes=[pltpu.SemaphoreType.DMA(())],
        name="route_invert",
    )(dest, vals, jnp.full((n_slots,), fill_value, I32))


def _moe_routing(meta, counts, n_tokens, n_experts):
    tm = MOE_TILE
    n_pairs = TOP_K_EXPERTS * n_tokens
    nt = n_pairs // tm + n_experts
    ei = meta[0:TOP_K_EXPERTS].T
    tw = lax.bitcast_convert_type(meta[TOP_K_EXPERTS:2 * TOP_K_EXPERTS], F32).T
    rk = meta[2 * TOP_K_EXPERTS:3 * TOP_K_EXPERTS].T
    cnt = counts[:, 0]
    tiles = (cnt + tm - 1) // tm
    tile_end = jnp.cumsum(tiles)
    group_start = (tile_end - tiles) * tm
    starts = jnp.sum(jnp.where(ei[..., None] == jnp.arange(n_experts, dtype=I32),
                               group_start.astype(I32), 0), axis=-1)
    dest = (starts + rk).reshape(-1)
    tok_bits = int(n_tokens).bit_length()
    low = (1 << tok_bits) - 1
    wbits = lax.bitcast_convert_type(tw, I32).reshape(-1)
    wbits = (wbits + (1 << (tok_bits - 1))) & ~low
    packed = wbits | (jnp.arange(n_pairs, dtype=I32) // TOP_K_EXPERTS)
    slots = _route_invert_call(dest, packed, nt * tm, n_tokens)
    tok_sorted = slots & low
    gate_sorted = lax.bitcast_convert_type(slots & ~low, F32)
    n_valid = tile_end[-1:].astype(I32)
    tid = jnp.minimum(jnp.arange(nt, dtype=I32), n_valid[0] - 1)
    tile_expert = jnp.sum((tile_end[None, :] <= tid[:, None]).astype(I32), axis=1)
    return tile_expert, n_valid, tok_sorted, gate_sorted


def _moe_res_ln_kernel(o_ref_in, x1_ref, g2_ref, lg_ref, lb_ref, o_ref, *, alpha):
    tc = x1_ref.shape[0]
    spt = o_ref_in.shape[0] // tc
    los, his = [], []
    for s in range(spt):
        lo, hi = _unpack_bf16_pair(o_ref_in[pl.ds(s, tc, stride=spt), :])
        los.append(lo)
        his.append(hi)
    moe = jnp.concatenate(los + his, axis=1)
    r = alpha * x1_ref[...] + (1.0 + g2_ref[0]) * moe
    o_ref[...] = _layer_norm(r, lg_ref[0], lb_ref[0])


def _moe_res_ln_call(o_packed, x1, mods, g2_row, ln, ln_row, S, alpha):
    T, D = x1.shape
    spt = (D // 2) // LANES
    tc = ROW_TILE
    tpb = S // tc
    lnspec = pl.BlockSpec((1, 1, D), lambda i: (ln_row, 0, 0))
    return pl.pallas_call(
        functools.partial(_moe_res_ln_kernel, alpha=alpha),
        grid=(T // tc,),
        in_specs=[pl.BlockSpec((tc * spt, LANES), lambda i: (i, 0)),
                  pl.BlockSpec((tc, D), lambda i: (i, 0)),
                  pl.BlockSpec((1, 1, D), lambda i: (g2_row(i // tpb), 0, 0)),
                  lnspec, lnspec],
        out_specs=pl.BlockSpec((tc, D), lambda i: (i, 0)),
        out_shape=jax.ShapeDtypeStruct((T, D), F32),
        compiler_params=_params(40, ("arbitrary",)),
        name="moe_res_ln",
    )(o_packed, x1, mods, ln[0], ln[1])


def _q_proj_kernel(x_ref, sh_ref, sc_ref, wt_ref, o_ref, *, out_scale):
    h = (x_ref[...] * (1.0 + sc_ref[0]) + sh_ref[0]).astype(BF16)
    zt = lax.dot_general(wt_ref[0], h, (((1,), (1,)), ((), ())), preferred_element_type=F32)
    o_ref[0] = (zt * out_scale).astype(BF16)


def _q_proj_call(x, mods, sh_row, sc_row, w_t, l, S, out_scale):
    T, D = x.shape
    N = w_t.shape[1]
    tm = MOBA_BLOCK
    tpb = S // tm
    mod = lambda row: pl.BlockSpec((1, 1, D), lambda i: (row(i // tpb), 0, 0))
    return pl.pallas_call(
        functools.partial(_q_proj_kernel, out_scale=out_scale),
        grid=(T // tm,),
        in_specs=[pl.BlockSpec((tm, D), lambda i: (i, 0)), mod(sh_row), mod(sc_row),
                  _layer_resident((N, D), l)],
        out_specs=pl.BlockSpec((1, N, tm), lambda i: (i, 0, 0)),
        out_shape=jax.ShapeDtypeStruct((T // tm, N, tm), BF16),
        compiler_params=_params(48, ("arbitrary",)),
        name="q_proj",
    )(x, mods, mods, w_t)


def _kv_proj_kernel(x_ref, sh_ref, sc_ref, wk_ref, wvt_ref, k_ref, vt_ref, km_ref):
    h = (x_ref[...] * (1.0 + sc_ref[0]) + sh_ref[0]).astype(BF16)
    zk = jnp.dot(h, wk_ref[...], preferred_element_type=F32)
    k_ref[...] = zk.astype(BF16)
    km_ref[0] = jnp.mean(zk, axis=0, keepdims=True)
    zvt = lax.dot_general(wvt_ref[...], h, (((1,), (1,)), ((), ())),
                          preferred_element_type=F32)
    vt_ref[0] = zvt.astype(BF16)


def _kv_proj_call(x, mods, sh_row, sc_row, w_k, w_vt, S):
    T, D = x.shape
    HD = w_k.shape[1]
    tm = MOBA_BLOCK
    tpb = S // tm
    mod = lambda row: pl.BlockSpec((1, 1, D), lambda i: (row(i // tpb), 0, 0))
    return pl.pallas_call(
        _kv_proj_kernel,
        grid=(T // tm,),
        in_specs=[pl.BlockSpec((tm, D), lambda i: (i, 0)), mod(sh_row), mod(sc_row),
                  _resident((D, HD)), _resident((HD, D))],
        out_specs=[pl.BlockSpec((tm, HD), lambda i: (i, 0)),
                   pl.BlockSpec((1, HD, tm), lambda i: (i, 0, 0)),
                   pl.BlockSpec((1, 1, HD), lambda i: (i, 0, 0))],
        out_shape=[jax.ShapeDtypeStruct((T, HD), BF16),
                   jax.ShapeDtypeStruct((T // tm, HD, tm), BF16),
                   jax.ShapeDtypeStruct((T // tm, 1, HD), F32)],
        compiler_params=_params(48, ("arbitrary",)),
        name="kv_proj",
    )(x, mods, mods, w_k, w_vt)


def _moba_kernel(qt_ref, k_ref, vt_ref, km_ref, o_ref, m_scr, l_scr, acc_scr, s_scr, *, hps):
    bs = qt_ref.shape[2]
    nb = km_ref.shape[1]
    qb = pl.program_id(2)
    start = pl.multiple_of(qb * bs, bs)
    kidx = lax.broadcasted_iota(I32, (bs, bs), 0)
    qidx = lax.broadcasted_iota(I32, (bs, bs), 1)
    blk = lax.broadcasted_iota(I32, (nb, bs), 0)
    valid = blk < qb
    heads = [slice(h * HEAD_DIM, (h + 1) * HEAD_DIM) for h in range(hps)]

    biases = []
    for cs in heads:
        qt = qt_ref[0, cs, :]
        km = km_ref[0, :, cs]
        km_hi = km.astype(BF16)
        km_lo = (km - km_hi.astype(F32)).astype(BF16)
        gate = (jnp.dot(km_hi, qt, preferred_element_type=F32)
                + jnp.dot(km_lo, qt, preferred_element_type=F32))
        gate = jnp.where(valid, gate, -jnp.inf)
        beaten = jnp.zeros((nb, bs), F32)
        for jp in range(nb):
            row = gate[jp:jp + 1, :]
            wins = jnp.logical_or(row > gate, jnp.logical_and(row == gate, jp < blk))
            beaten = beaten + jnp.where(wins, 1.0, 0.0)
        biases.append(jnp.where(jnp.logical_and(valid, beaten < MOBA_TOPK), 0.0, MASK_VALUE))

    def scores(rows, slot):
        for h, cs in enumerate(heads):
            s_scr[slot, h] = jnp.dot(k_ref[rows, cs], qt_ref[0, cs, :],
                                     preferred_element_type=F32)

    own_slot = 1
    scores(pl.ds(start, bs), own_slot)
    scores(slice(0, bs), 0)

    for h, cs in enumerate(heads):
        s = jnp.where(kidx <= qidx, s_scr[own_slot, h], MASK_VALUE)
        m = jnp.max(s, axis=0, keepdims=True)
        p = jnp.exp2(s - m)
        m_scr[h] = m
        l_scr[h] = jnp.sum(p, axis=0, keepdims=True)
        acc_scr[h] = jnp.dot(vt_ref[qb, cs, :], p.astype(BF16), preferred_element_type=F32)

    for j in range(nb - 1):
        @pl.when(j < qb)
        def _(j=j):
            if j + 1 < nb - 1:
                scores(slice((j + 1) * bs, (j + 2) * bs), (j + 1) % 2)
            for h, cs in enumerate(heads):
                bias = biases[h][j:j + 1, :]
                m_old = m_scr[h]
                m_new = jnp.maximum(
                    m_old, jnp.max(s_scr[j % 2, h], axis=0, keepdims=True) + bias)
                a = jnp.exp2(m_old - m_new)
                shift = m_new - bias
                p = jnp.exp2(s_scr[j % 2, h] - shift)
                l_scr[h] = a * l_scr[h] + jnp.sum(p, axis=0, keepdims=True)
                acc_scr[h] = a * acc_scr[h] + jnp.dot(
                    vt_ref[j, cs, :], p.astype(BF16), preferred_element_type=F32)
                m_scr[h] = m_new

    for h, cs in enumerate(heads):
        o_ref[:, cs] = (acc_scr[h] * (1.0 / l_scr[h])).T.astype(BF16)


def _moba_call(qt, k, vt, kmeans, B, S, hps):
    T, HD = k.shape
    H = HD // HEAD_DIM
    bs = MOBA_BLOCK
    nb = S // bs
    hw = hps * HEAD_DIM
    return pl.pallas_call(
        functools.partial(_moba_kernel, hps=hps),
        grid=(B, H // hps, nb),
        in_specs=[
            pl.BlockSpec((1, hw, bs), lambda b, g, i: (b * nb + i, g, 0)),
            pl.BlockSpec((S, hw), lambda b, g, i: (b, g)),
            pl.BlockSpec((nb, hw, bs), lambda b, g, i: (b, g, 0)),
            pl.BlockSpec((1, nb, hw), lambda b, g, i: (b, 0, g)),
        ],
        out_specs=pl.BlockSpec((bs, hw), lambda b, g, i: (b * nb + i, g)),
        out_shape=jax.ShapeDtypeStruct((T, HD), BF16),
        scratch_shapes=[pltpu.VMEM((hps, 1, bs), F32), pltpu.VMEM((hps, 1, bs), F32),
                        pltpu.VMEM((hps, HEAD_DIM, bs), F32),
                        pltpu.VMEM((2, hps, bs, bs), F32)],
        compiler_params=_params(48, ("arbitrary", "arbitrary", "arbitrary")),
        name="moba_attention",
    )(qt, k, vt, kmeans.reshape(B, nb, HD))


def kernel(x, c, ada_w, ada_b, ln_g, ln_b, gm_w_in, gm_b_in, gm_lnv_g, gm_lnv_b, gm_w_s,
           gm_b_s, gm_w_out, kv_ada_w, kv_ada_b, w_kv, attn_w_q, attn_w_out, moe_w_router,
           moe_b_router, moe_w_gu, moe_b_gu, moe_w_down, moe_b_down):
    B, S, D = x.shape
    T = B * S
    depth = ada_w.shape[0]
    n_a = gm_w_in.shape[0]
    E = moe_w_router.shape[2]
    W = gm_w_in.shape[2] // 2
    C = gm_w_s.shape[2]
    H = attn_w_q.shape[2] // HEAD_DIM
    hps = next(n for n in (8, 4, 2, 1) if H % n == 0)
    assert B <= MODS_PAD_ROWS and S % MOBA_BLOCK == 0 and S % ROW_TILE == 0
    assert ROW_TILE % C == 0 and W // gm_w_s.shape[1] == LANES
    assert (D // 2) % LANES == 0 and (TOP_K_EXPERTS * T) % MOE_TILE == 0
    alpha = (2.0 * depth) ** 0.25

    c_pad = jnp.zeros((MODS_PAD_ROWS, D), F32).at[:B].set(c)
    n_mod = ada_w.shape[2] // D
    mods = _mods_call(c_pad, ada_w, ada_b).reshape(depth * MODS_PAD_ROWS * n_mod, 1, D)
    kv_mods = _mods_call(c_pad, kv_ada_w[None], kv_ada_b[None]).reshape(
        MODS_PAD_ROWS * 2, 1, D)

    def mod_row(l, which):
        return lambda b: (l * MODS_PAD_ROWS + b) * n_mod + which

    ln_g3 = ln_g.reshape(depth * 2, 1, D)
    ln_b3 = ln_b.reshape(depth * 2, 1, D)
    causal = jnp.tril(jnp.ones((C, C), F32))
    gm_w_in_b = gm_w_in.astype(BF16)
    gm_w_out_b = gm_w_out.astype(BF16)
    gm_ws_b = (gm_w_s * causal).astype(BF16)
    gm_bs_t = jnp.swapaxes(gm_b_s, 1, 2)
    attn_w_qt_b = jnp.swapaxes(attn_w_q, 1, 2).astype(BF16)
    attn_w_out_b = attn_w_out.astype(BF16)
    w_rt_b = jnp.swapaxes(moe_w_router, 1, 2).astype(BF16)
    w_gu_b = moe_w_gu.astype(BF16)

    xt = x.reshape(T, D)
    k = vt = kmeans = None
    for l in range(depth):
        if l < n_a:
            y = _gmlp_in_call(
                xt, mods, mod_row(l, 0), mod_row(l, 1), gm_w_in_b, gm_b_in[:, None],
                gm_lnv_g[:, None], gm_lnv_b[:, None], gm_ws_b, gm_bs_t, l, S)
            w_o, lw = gm_w_out_b, l
        else:
            lw = l - n_a
            qt = _q_proj_call(xt, mods, mod_row(l, 0), mod_row(l, 1), attn_w_qt_b, lw, S,
                              HEAD_DIM ** -0.5 * LOG2_E)
            y = _moba_call(qt, k, vt, kmeans, B, S, hps)
            w_o = attn_w_out_b
        x1, xp, meta, counts = _proj_ln_router_call(
            y, w_o, lw, xt, mods, mod_row(l, 2), mod_row(l, 3), mod_row(l, 4),
            (ln_g3, ln_b3), 2 * l, w_rt_b, moe_b_router[:, :, None], l, S, alpha)
        tile_expert, n_valid, tok_sorted, gate_sorted = _moe_routing(meta, counts, T, E)
        act = _moe_up_call(tile_expert, n_valid, tok_sorted, xp, w_gu_b, moe_b_gu, l)
        moe_out = _moe_down_call(tile_expert, n_valid, tok_sorted, gate_sorted, act,
                                 moe_w_down, moe_b_down, l, T)
        xt = _moe_res_ln_call(moe_out, x1, mods, mod_row(l, 5), (ln_g3, ln_b3), 2 * l + 1, S,
                              alpha)
        if l == n_a - 1:
            hd = w_kv.shape[1] // 2
            k, vt, kmeans = _kv_proj_call(
                xt, kv_mods, lambda b: b * 2, lambda b: b * 2 + 1,
                w_kv[:, :hd].astype(BF16), w_kv[:, hd:].T.astype(BF16), S)
    return xt.reshape(B, S, D)
```
